```python
import math
import jax, jax.numpy as jnp
from jax import lax
import numpy as np

D_MODEL = 2048
BATCH = 8
SEQ = 2048
DEPTH = 2

GRID_W = 64
CTX_LEN = 256

NA_HEADS = 8
NA_HEAD_DIM = 128
NA_WIN_ROWS = 8
NA_WIN_COLS = 16

RET_HEADS = 8
RET_KEY_DIM = 128
RET_VAL_DIM = 256
RET_CHUNK = 128

ROPE_BASE = 10000.0
ROPE_FREQS_PER_AXIS = RET_KEY_DIM // 4
NORM_EPS = 1e-6
MASK_VALUE = -1e30

W_NA = NA_HEADS * NA_HEAD_DIM
W_RET_QK = RET_HEADS * RET_KEY_DIM
W_RET_V = RET_HEADS * RET_VAL_DIM

NA_Q, NA_K, NA_V, NA_Z, RET_Q, RET_K, RET_V, RET_Z, G_NA, G_RET = range(10)
SPLIT_SIZES = (W_NA, W_NA, W_NA, W_NA, W_RET_QK, W_RET_QK, W_RET_V, W_RET_V, D_MODEL, D_MODEL)
SPLIT_OFFSETS = tuple(int(o) for o in np.cumsum((0,) + SPLIT_SIZES))
N_SPLITS = len(SPLIT_SIZES)
IN_COLS = SPLIT_OFFSETS[-1]

kernel_name = 'hybrid_na_retention_dit'


def rmsnorm(x, g):
    xf = x.astype(jnp.float32)
    y = xf * lax.rsqrt(jnp.mean(xf * xf, axis=-1, keepdims=True) + NORM_EPS)
    return (y * g.astype(jnp.float32)).astype(x.dtype)


def to_heads(t, n_heads):
    b, n, w = t.shape
    return t.reshape(b, n, n_heads, w // n_heads).transpose(0, 2, 1, 3)


def from_heads(t):
    b, h, n, d = t.shape
    return t.transpose(0, 2, 1, 3).reshape(b, n, h * d)


def in_block(t, i):
    return t[..., SPLIT_OFFSETS[i]:SPLIT_OFFSETS[i + 1]]


def axial_rope(n_tokens, dtype):
    t = jnp.arange(n_tokens)
    row = (t // GRID_W).astype(jnp.float32)
    col = (t % GRID_W).astype(jnp.float32)
    inv_freq = ROPE_BASE ** (-jnp.arange(ROPE_FREQS_PER_AXIS, dtype=jnp.float32) / ROPE_FREQS_PER_AXIS)
    ang = jnp.concatenate([row[:, None] * inv_freq, col[:, None] * inv_freq], axis=-1)
    return jnp.cos(ang).astype(dtype), jnp.sin(ang).astype(dtype)


def apply_rope(x, cos, sin):
    half = x.shape[-1] // 2
    x1, x2 = x[..., :half], x[..., half:]
    return jnp.concatenate([x1 * cos - x2 * sin, x2 * cos + x1 * sin], axis=-1)


def neighbourhood_attention(q, k, v, k_ctx, v_ctx, rpb):
    b, h, s, d = q.shape
    rows = s // GRID_W
    kh = min(NA_WIN_ROWS, rows)
    kw = NA_WIN_COLS
    r = jnp.arange(rows)
    cidx = jnp.arange(GRID_W)
    r0 = jnp.clip(r - kh // 2, 0, rows - kh)
    row_idx = r0[:, None] + jnp.arange(kh)[None, :]
    c0 = jnp.clip(cidx - kw // 2, 0, GRID_W - kw)
    col_in = (cidx[None, :] >= c0[:, None]) & (cidx[None, :] < c0[:, None] + kw)
    qg = q.reshape(b, h, rows, GRID_W, d)
    kg = jnp.take(k.reshape(b, h, rows, GRID_W, d), row_idx, axis=2)
    vg = jnp.take(v.reshape(b, h, rows, GRID_W, d), row_idx, axis=2)
    scale = d ** -0.5
    s_loc = jnp.einsum('bhrcd,bhrkwd->bhrckw', qg, kg).astype(jnp.float32) * scale
    dr = row_idx - r[:, None] + (NA_WIN_ROWS - 1)
    dc = jnp.clip(cidx[None, :] - cidx[:, None] + (NA_WIN_COLS - 1), 0, 2 * NA_WIN_COLS - 2)
    bias = rpb[:, dr[:, None, :, None], dc[None, :, None, :]].astype(jnp.float32)
    s_loc = jnp.where(col_in[None, None, None, :, None, :], s_loc + bias[None], MASK_VALUE)
    s_ctx = jnp.einsum('bhrcd,bhld->bhrcl', qg, k_ctx).astype(jnp.float32) * scale
    n_loc = kh * GRID_W
    scores = jnp.concatenate([s_loc.reshape(b, h, rows, GRID_W, n_loc), s_ctx], axis=-1)
    p = jax.nn.softmax(scores, axis=-1).astype(v.dtype)
    p_loc = p[..., :n_loc].reshape(b, h, rows, GRID_W, kh, GRID_W)
    p_ctx = p[..., n_loc:]
    out = (jnp.einsum('bhrckw,bhrkwd->bhrcd', p_loc, vg)
           + jnp.einsum('bhrcl,bhld->bhrcd', p_ctx, v_ctx))
    return out.reshape(b, h, s, d)


def context_attention(q, k, v):
    s = jnp.einsum('bhqd,bhkd->bhqk', q, k).astype(jnp.float32) * (q.shape[-1] ** -0.5)
    p = jax.nn.softmax(s, axis=-1).astype(v.dtype)
    return jnp.einsum('bhqk,bhkd->bhqd', p, v)


def retention_chunkwise(q, k, v, log_gamma, state0):
    b, h, n, dk = q.shape
    dv = v.shape[-1]
    cs = RET_CHUNK
    nc = n // cs
    pos = jnp.arange(cs, dtype=jnp.float32)
    diff = pos[:, None] - pos[None, :]
    intra = jnp.where(diff >= 0, jnp.exp(jnp.maximum(diff, 0.0) * log_gamma[:, None, None]), 0.0)
    q_dec = jnp.exp((pos + 1.0)[None, :] * log_gamma[:, None])
    k_dec = jnp.exp((cs - 1.0 - pos)[None, :] * log_gamma[:, None])
    chunk_dec = jnp.exp(cs * log_gamma)
    qc = q.reshape(b, h, nc, cs, dk)
    kc = k.reshape(b, h, nc, cs, dk)
    vc = v.reshape(b, h, nc, cs, dv)
    scores = jnp.einsum('bhnid,bhnjd->bhnij', qc, kc) * intra[None, :, None]
    inner = jnp.einsum('bhnij,bhnje->bhnie', scores, vc)

    def step(state, xs):
        q_i, k_i, v_i = xs
        cross = jnp.einsum('bhid,bhde->bhie', q_i * q_dec[None, :, :, None], state)
        state = (state * chunk_dec[None, :, None, None]
                 + jnp.einsum('bhjd,bhje->bhde', k_i * k_dec[None, :, :, None], v_i))
        return state, cross

    xs = (jnp.moveaxis(qc, 2, 0), jnp.moveaxis(kc, 2, 0), jnp.moveaxis(vc, 2, 0))
    state_final, cross = lax.scan(step, state0, xs)
    out = inner + jnp.moveaxis(cross, 0, 2)
    return out.reshape(b, h, n, dv), state_final


def bidir_retention(q, k, v, log_gamma, state_fwd, state_bwd):
    q, k, v = q.astype(jnp.float32), k.astype(jnp.float32), v.astype(jnp.float32)
    o_f, s_f = retention_chunkwise(q, k, v, log_gamma[0], state_fwd)
    o_b, s_b = retention_chunkwise(jnp.flip(q, 2), jnp.flip(k, 2), jnp.flip(v, 2), log_gamma[1], state_bwd)
    return o_f + jnp.flip(o_b, 2), s_f, s_b


def context_final_states(k, v, log_gamma):
    k, v = k.astype(jnp.float32), v.astype(jnp.float32)
    n = k.shape[2]
    pos = jnp.arange(n, dtype=jnp.float32)
    w_f = jnp.exp((n - 1.0 - pos)[None, :] * log_gamma[0][:, None])
    w_b = jnp.exp(pos[None, :] * log_gamma[1][:, None])
    s_f = jnp.einsum('bhld,bhle->bhde', k * w_f[None, :, :, None], v)
    s_b = jnp.einsum('bhld,bhle->bhde', k * w_b[None, :, :, None], v)
    return s_f, s_b


def ret_head_norm(o, dtype):
    o = o * lax.rsqrt(jnp.mean(o * o, axis=-1, keepdims=True) + NORM_EPS)
    return o.astype(dtype)


def merge_branches(o_na, o_ret, blocks, w_proj_na, w_proj_ret, w_out):
    dtype = blocks[NA_Z].dtype
    y_na = (from_heads(o_na) * jax.nn.silu(blocks[NA_Z])) @ w_proj_na
    y_ret = (from_heads(ret_head_norm(o_ret, dtype)) * jax.nn.silu(blocks[RET_Z])) @ w_proj_ret
    merged = jax.nn.sigmoid(blocks[G_NA]) * y_na + jax.nn.sigmoid(blocks[G_RET]) * y_ret
    return merged @ w_out


def hybrid_layer(x_lat, x_ctx, mod_lat, mod_ctx, norm_g, w_in, rpb, decay_logit,
                 w_proj_na, w_proj_ret, w_out, update_ctx):
    shift, scale, gate = jnp.split(mod_lat, 3, axis=-1)
    c_shift, c_scale, c_gate = jnp.split(mod_ctx, 3, axis=-1)
    h_lat = rmsnorm(x_lat, norm_g) * (1.0 + scale[:, None]) + shift[:, None]
    h_ctx = rmsnorm(x_ctx, norm_g) * (1.0 + c_scale) + c_shift
    log_gamma = jax.nn.log_sigmoid(decay_logit.astype(jnp.float32))

    u = h_lat @ w_in
    lat = {i: in_block(u, i) for i in range(N_SPLITS)}
    if update_ctx:
        uc = h_ctx @ w_in
        cb = {i: in_block(uc, i) for i in range(N_SPLITS)}
    else:
        cb = {i: h_ctx @ in_block(w_in, i) for i in (NA_K, NA_V, RET_K, RET_V)}

    k_na_ctx = to_heads(cb[NA_K], NA_HEADS)
    v_na_ctx = to_heads(cb[NA_V], NA_HEADS)
    o_na = neighbourhood_attention(to_heads(lat[NA_Q], NA_HEADS), to_heads(lat[NA_K], NA_HEADS),
                                   to_heads(lat[NA_V], NA_HEADS), k_na_ctx, v_na_ctx, rpb)

    k_scale = RET_KEY_DIM ** -0.5
    cos, sin = axial_rope(x_lat.shape[1], x_lat.dtype)
    q_ret = apply_rope(to_heads(lat[RET_Q], RET_HEADS), cos, sin)
    k_ret = apply_rope(to_heads(lat[RET_K], RET_HEADS), cos, sin) * k_scale
    v_ret = to_heads(lat[RET_V], RET_HEADS)
    k_ret_ctx = to_heads(cb[RET_K], RET_HEADS) * k_scale
    v_ret_ctx = to_heads(cb[RET_V], RET_HEADS)
    if update_ctx:
        b = x_ctx.shape[0]
        zeros = jnp.zeros((b, RET_HEADS, RET_KEY_DIM, RET_VAL_DIM), jnp.float32)
        o_ret_ctx, s_f, s_b = bidir_retention(to_heads(cb[RET_Q], RET_HEADS), k_ret_ctx, v_ret_ctx,
                                              log_gamma, zeros, zeros)
    else:
        s_f, s_b = context_final_states(k_ret_ctx, v_ret_ctx, log_gamma)
    o_ret, _, _ = bidir_retention(q_ret, k_ret, v_ret, log_gamma, s_f, s_b)

    out_lat = merge_branches(o_na, o_ret, lat, w_proj_na, w_proj_ret, w_out)
    x_lat = x_lat + gate[:, None] * out_lat
    if update_ctx:
        o_na_ctx = context_attention(to_heads(cb[NA_Q], NA_HEADS), k_na_ctx, v_na_ctx)
        out_ctx = merge_branches(o_na_ctx, o_ret_ctx, cb, w_proj_na, w_proj_ret, w_out)
        x_ctx = x_ctx + c_gate * out_ctx
    return x_lat, x_ctx


def setup_inputs(seed: int = 0) -> dict:
    key = jax.random.key(seed)
    ks = jax.random.split(key, 14)
    f32 = jnp.float32
    base_logit = jnp.log(2.0 ** (5.0 + jnp.arange(RET_HEADS, dtype=f32)) - 1.0)
    return {
        'x': jax.random.normal(ks[0], (BATCH, SEQ, D_MODEL), f32),
        'c': jax.random.normal(ks[1], (BATCH, D_MODEL), f32),
        'ctx': jax.random.normal(ks[2], (BATCH, CTX_LEN, D_MODEL), f32),
        'c_ctx': jax.random.normal(ks[3], (D_MODEL,), f32),
        'ada_w': jax.random.normal(ks[4], (DEPTH, D_MODEL, 3 * D_MODEL), f32) * D_MODEL ** -0.5,
        'ada_b': jax.random.normal(ks[5], (DEPTH, 3 * D_MODEL), f32) * 0.01,
        'norm_g': 1.0 + 0.1 * jax.random.normal(ks[6], (DEPTH, D_MODEL), f32),
        'w_in': jax.random.normal(ks[7], (DEPTH, D_MODEL, IN_COLS), f32) * D_MODEL ** -0.5,
        'na_rpb': 0.1 * jax.random.normal(ks[8], (DEPTH, NA_HEADS, 2 * NA_WIN_ROWS - 1, 2 * NA_WIN_COLS - 1), f32),
        'ret_decay_logit': base_logit[None, None, :] + 0.1 * jax.random.normal(ks[9], (DEPTH, 2, RET_HEADS), f32),
        'w_proj_na': jax.random.normal(ks[10], (DEPTH, W_NA, D_MODEL), f32) * W_NA ** -0.5,
        'w_proj_ret': jax.random.normal(ks[11], (DEPTH, W_RET_V, D_MODEL), f32) * W_RET_V ** -0.5,
        'w_out': jax.random.normal(ks[12], (DEPTH, D_MODEL, D_MODEL), f32) * D_MODEL ** -0.5,
        'final_g': 1.0 + 0.1 * jax.random.normal(ks[13], (D_MODEL,), f32),
    }


def reference(x, c, ctx, c_ctx, ada_w, ada_b, norm_g, w_in, na_rpb, ret_decay_logit,
              w_proj_na, w_proj_ret, w_out, final_g):
    c_silu = jax.nn.silu(c)
    cc_silu = jax.nn.silu(c_ctx)
    x_lat, x_ctx = x, ctx
    for l in range(DEPTH):
        mod_lat = c_silu @ ada_w[l] + ada_b[l]
        mod_ctx = cc_silu @ ada_w[l] + ada_b[l]
        x_lat, x_ctx = hybrid_layer(x_lat, x_ctx, mod_lat, mod_ctx, norm_g[l], w_in[l], na_rpb[l],
                                    ret_decay_logit[l], w_proj_na[l], w_proj_ret[l], w_out[l],
                                    update_ctx=(l < DEPTH - 1))
    return rmsnorm(x_lat, final_g)
```

```python
import functools

import jax
import jax.numpy as jnp
from jax import lax
from jax.experimental import pallas as pl
from jax.experimental.pallas import tpu as pltpu

F32 = jnp.float32
BF16 = jnp.bfloat16

D_MODEL = 2048
GRID_W = 64
N_HEADS = 8
NA_DIM = 128
NA_WIN_ROWS = 8
NA_WIN_COLS = 16
RET_KEY_DIM = 128
RET_VAL_DIM = 256
RET_CHUNK = 128
ROPE_BASE = 10000.0
NORM_EPS = 1e-6
MASK_VALUE = -1e30

OFF_NA_Q, OFF_NA_K, OFF_NA_V, OFF_NA_Z = 0, 1024, 2048, 3072
OFF_RET_Q, OFF_RET_K, OFF_RET_V, OFF_RET_Z = 4096, 5120, 6144, 8192
OFF_G_NA, OFF_G_RET = 10240, 12288
IN_COLS = 14336
CTX_KV_COLS = 5120
COFF_NA_K, COFF_NA_V, COFF_RET_K, COFF_RET_V = 0, 1024, 2048, 3072

MOD_ROWS = 16
CTX_MOD_ROW = 8

NA_QBLK = 4 * GRID_W
NA_KBLK = 12 * GRID_W

VMEM_LIMIT = 56 * 1024 * 1024


def _dot(a, b):
    return jnp.dot(a, b, preferred_element_type=F32)


def _dot_nt(a, b):
    return lax.dot_general(a, b, (((1,), (1,)), ((), ())), preferred_element_type=F32)


def _dot_tn(a, b):
    return lax.dot_general(a, b, (((0,), (0,)), ((), ())), preferred_element_type=F32)


def _sigmoid(x):
    return 1.0 / (1.0 + jnp.exp(-x))


def _mod_kernel(c_ref, w_ref, b_ref, o_ref):
    c = c_ref[...]
    cs = (c * _sigmoid(c)).astype(BF16)
    o_ref[0] = _dot(cs, w_ref[0].astype(BF16)) + b_ref[0]


def _modulation(c_all, ada_w, ada_b, tn=512):
    depth, d, n = ada_w.shape
    return pl.pallas_call(
        _mod_kernel,
        grid=(depth, n // tn),
        in_specs=[
            pl.BlockSpec((MOD_ROWS, d), lambda l, j: (0, 0)),
            pl.BlockSpec((1, d, tn), lambda l, j: (l, 0, j)),
            pl.BlockSpec((1, 1, tn), lambda l, j: (l, 0, j)),
        ],
        out_specs=pl.BlockSpec((1, MOD_ROWS, tn), lambda l, j: (l, 0, j)),
        out_shape=jax.ShapeDtypeStruct((depth, MOD_ROWS, n), F32),
        compiler_params=pltpu.CompilerParams(
            dimension_semantics=("arbitrary", "arbitrary"), vmem_limit_bytes=VMEM_LIMIT),
        name="adaln_mod",
    )(c_all, ada_w, ada_b.reshape(depth, 1, n))


def _inproj_kernel(x_ref, shift_ref, scale_ref, g_ref, w_ref, o_ref, h_ref, *, row_chunk):
    @pl.when(pl.program_id(1) == 0)
    def _():
        def body(r, carry):
            sl = pl.ds(pl.multiple_of(r * row_chunk, row_chunk), row_chunk)
            x = x_ref[sl, :]
            ms = jnp.mean(x * x, axis=-1, keepdims=True)
            y = x * lax.rsqrt(ms + NORM_EPS) * g_ref[...]
            h = y * (1.0 + scale_ref[0]) + shift_ref[0]
            h_ref[sl, :] = h.astype(BF16)
            return carry
        lax.fori_loop(0, x_ref.shape[0] // row_chunk, body, 0)

    o_ref[...] = _dot(h_ref[...], w_ref[...]).astype(o_ref.dtype)


def _inproj(x2, shift, scale, g, w, *, rows_per_mod, mod_row, col_map, n_out, tm, tn):
    m, d = x2.shape
    if mod_row is None:
        mod_idx = lambda i, j: ((i * tm) // rows_per_mod, 0, 0)
    else:
        mod_idx = lambda i, j: (mod_row, 0, 0)
    return pl.pallas_call(
        functools.partial(_inproj_kernel, row_chunk=128),
        grid=(m // tm, n_out // tn),
        in_specs=[
            pl.BlockSpec((tm, d), lambda i, j: (i, 0)),
            pl.BlockSpec((1, 1, d), mod_idx),
            pl.BlockSpec((1, 1, d), mod_idx),
            pl.BlockSpec((1, d), lambda i, j: (0, 0)),
            pl.BlockSpec((d, tn), lambda i, j: (0, col_map(j))),
        ],
        out_specs=pl.BlockSpec((tm, tn), lambda i, j: (i, j)),
        out_shape=jax.ShapeDtypeStruct((m, n_out), BF16),
        scratch_shapes=[pltpu.VMEM((tm, d), BF16)],
        compiler_params=pltpu.CompilerParams(
            dimension_semantics=("arbitrary", "arbitrary"), vmem_limit_bytes=VMEM_LIMIT),
        name="inproj",
    )(x2, shift, scale, g, w)


def _na_window_row0(r, rows):
    return min(max(r - NA_WIN_ROWS // 2, 0), rows - NA_WIN_ROWS)


def _na_key_row0(i, rows):
    return min(max(4 * i - 4, 0), rows - 12)


def _na_kernel(*refs, rows, with_ctx_q):
    if with_ctx_q:
        pb_ref, q_ref, k_ref, v_ref, kc_ref, vc_ref, qc_ref, o_ref, oc_ref, bias_ref = refs
    else:
        pb_ref, q_ref, k_ref, v_ref, kc_ref, vc_ref, o_ref, bias_ref = refs
    n_blk = rows // 4
    scale = NA_DIM ** -0.5

    @pl.when(pl.program_id(1) == 0)
    def _():
        lane = lax.broadcasted_iota(jnp.int32, (GRID_W, 2 * GRID_W), 1)
        neg = jnp.full((GRID_W, 2 * GRID_W), MASK_VALUE, F32)
        for ti, i in enumerate((0, 1, n_blk - 1)):
            krow0 = _na_key_row0(i, rows)
            for rl in range(4):
                r = 4 * i + rl
                r0 = _na_window_row0(r, rows)
                for j in range(6):
                    ka = krow0 + 2 * j
                    in_a = r0 <= ka < r0 + NA_WIN_ROWS
                    in_b = r0 <= ka + 1 < r0 + NA_WIN_ROWS
                    if in_a or in_b:
                        t = pb_ref[0, ka - r + NA_WIN_ROWS]
                        if not in_b:
                            t = jnp.where(lane < GRID_W, t, neg)
                        if not in_a:
                            t = jnp.where(lane >= GRID_W, t, neg)
                    else:
                        t = neg
                    bias_ref[ti, rl * GRID_W:(rl + 1) * GRID_W, j * 128:(j + 1) * 128] = t

    kc = kc_ref[0]
    vc = vc_ref[0]

    def body(i, carry):
        qsl = pl.ds(pl.multiple_of(i * NA_QBLK, NA_QBLK), NA_QBLK)
        kstart = pl.multiple_of(jnp.clip(i - 1, 0, n_blk - 3) * NA_QBLK, NA_QBLK)
        ksl = pl.ds(kstart, NA_KBLK)
        ti = jnp.where(i == 0, 0, jnp.where(i == n_blk - 1, 2, 1))
        q = q_ref[0, qsl, :]
        s = _dot_nt(q, k_ref[0, ksl, :]) * scale + bias_ref[ti]
        sc = _dot_nt(q, kc) * scale
        m = jnp.maximum(jnp.max(s, axis=-1, keepdims=True), jnp.max(sc, axis=-1, keepdims=True))
        p = jnp.exp(s - m)
        pc = jnp.exp(sc - m)
        l = jnp.sum(p, axis=-1, keepdims=True) + jnp.sum(pc, axis=-1, keepdims=True)
        o = _dot(p.astype(BF16), v_ref[0, ksl, :]) + _dot(pc.astype(BF16), vc)
        o_ref[0, qsl, :] = (o / l).astype(o_ref.dtype)
        return carry

    lax.fori_loop(0, n_blk, body, 0)

    if with_ctx_q:
        sc = _dot_nt(qc_ref[0], kc) * scale
        m = jnp.max(sc, axis=-1, keepdims=True)
        pc = jnp.exp(sc - m)
        l = jnp.sum(pc, axis=-1, keepdims=True)
        oc_ref[0] = (_dot(pc.astype(BF16), vc) / l).astype(oc_ref.dtype)


def _na_attention(pb, u3, uc3, *, ctx_k_off, ctx_v_off, ctx_q_off):
    b, s, _ = u3.shape
    lc = uc3.shape[1]
    rows = s // GRID_W
    with_ctx_q = ctx_q_off is not None
    hb = lambda off: off // NA_DIM
    in_specs = [
        pl.BlockSpec((1,) + pb.shape[1:], lambda h, bb: (h, 0, 0, 0)),
        pl.BlockSpec((1, s, NA_DIM), lambda h, bb: (bb, 0, hb(OFF_NA_Q) + h)),
        pl.BlockSpec((1, s, NA_DIM), lambda h, bb: (bb, 0, hb(OFF_NA_K) + h)),
        pl.BlockSpec((1, s, NA_DIM), lambda h, bb: (bb, 0, hb(OFF_NA_V) + h)),
        pl.BlockSpec((1, lc, NA_DIM), lambda h, bb: (bb, 0, hb(ctx_k_off) + h)),
        pl.BlockSpec((1, lc, NA_DIM), lambda h, bb: (bb, 0, hb(ctx_v_off) + h)),
    ]
    args = [pb, u3, u3, u3, uc3, uc3]
    out_specs = [pl.BlockSpec((1, s, NA_DIM), lambda h, bb: (bb, 0, h))]
    out_shape = [jax.ShapeDtypeStruct((b, s, N_HEADS * NA_DIM), BF16)]
    if with_ctx_q:
        in_specs.append(pl.BlockSpec((1, lc, NA_DIM), lambda h, bb: (bb, 0, hb(ctx_q_off) + h)))
        args.append(uc3)
        out_specs.append(pl.BlockSpec((1, lc, NA_DIM), lambda h, bb: (bb, 0, h)))
        out_shape.append(jax.ShapeDtypeStruct((b, lc, N_HEADS * NA_DIM), BF16))
    outs = pl.pallas_call(
        functools.partial(_na_kernel, rows=rows, with_ctx_q=with_ctx_q),
        grid=(N_HEADS, b),
        in_specs=in_specs,
        out_specs=out_specs,
        out_shape=out_shape,
        scratch_shapes=[pltpu.VMEM((3, NA_QBLK, NA_KBLK), F32)],
        compiler_params=pltpu.CompilerParams(
            dimension_semantics=("arbitrary", "arbitrary"), vmem_limit_bytes=VMEM_LIMIT),
        name="na_attention",
    )(*args)
    return outs if with_ctx_q else (outs[0], None)


def _na_pair_bias(rpb):
    cq = jnp.arange(GRID_W)[:, None]
    ck = jnp.arange(GRID_W)[None, :]
    c0 = jnp.clip(cq - NA_WIN_COLS // 2, 0, GRID_W - NA_WIN_COLS)
    col_in = (ck >= c0) & (ck < c0 + NA_WIN_COLS)
    dc = jnp.clip(ck - cq + (NA_WIN_COLS - 1), 0, 2 * NA_WIN_COLS - 2)
    onehot = ((dc[None] == jnp.arange(2 * NA_WIN_COLS - 1)[:, None, None]) & col_in[None]).astype(F32)
    cb = jnp.einsum('hrd,dqk->hrqk', rpb.astype(F32), onehot, precision=lax.Precision.HIGHEST)
    cb = jnp.where(col_in[None, None], cb, MASK_VALUE)
    neg = jnp.full_like(cb[:, :1], MASK_VALUE)
    cbx = jnp.concatenate([neg, cb, neg], axis=1)
    return jnp.concatenate([cbx[:, :-1], cbx[:, 1:]], axis=-1)


def _ret_kernel(*refs, with_ctx_q):
    if with_ctx_q:
        (dl_ref, cos_ref, sin_ref, q_ref, k_ref, v_ref, kc_ref, vc_ref, qc_ref,
         o_ref, oc_ref, kv_ref, st_ref, kr_ref) = refs
    else:
        (dl_ref, cos_ref, sin_ref, q_ref, k_ref, v_ref, kc_ref, vc_ref,
         o_ref, kv_ref, st_ref, kr_ref) = refs
        qc_ref = oc_ref = None
    cs = RET_CHUNK
    dk = RET_KEY_DIM
    k_scale = RET_KEY_DIM ** -0.5

    dl = dl_ref[...]
    lg = -(jnp.maximum(-dl, 0.0) + jnp.log1p(jnp.exp(-jnp.abs(dl))))
    head = lax.broadcasted_iota(jnp.int32, dl.shape, 1)
    lgh = jnp.sum(jnp.where(head == pl.program_id(1), lg, 0.0), axis=1, keepdims=True)
    lgf = lgh[0:1, :]
    lgb = lgh[1:2, :]

    pos = lax.broadcasted_iota(jnp.int32, (cs, dk), 0).astype(F32)
    q_dec_f = jnp.exp((pos + 1.0) * lgf)
    q_dec_b = jnp.exp((cs - pos) * lgb)
    k_dec_f = jnp.exp((cs - 1.0 - pos) * lgf)
    k_dec_b = jnp.exp(pos * lgb)
    dec_f = jnp.exp(cs * lgf)
    dec_b = jnp.exp(cs * lgb)
    ii = lax.broadcasted_iota(jnp.int32, (cs, cs), 0)
    jj = lax.broadcasted_iota(jnp.int32, (cs, cs), 1)
    diff = (ii - jj).astype(F32)
    intra = jnp.where(diff > 0, jnp.exp(jnp.maximum(diff, 0.0) * lgf),
                      jnp.where(diff < 0, jnp.exp(jnp.maximum(-diff, 0.0) * lgb), 2.0))

    def rope(t, sl, use_rope):
        if not use_rope:
            return t
        return t * cos_ref[sl, :] + pltpu.roll(t, dk // 2, 1) * sin_ref[sl, :]

    def run(qr, kr, vr, outr, n_tok, use_rope, s0f, s0b):
        nc = n_tok // cs

        def chunk(t):
            return pl.ds(pl.multiple_of(t * cs, cs), cs)

        def kv_body(t, carry):
            sl = chunk(t)
            k = rope(kr[0, sl, :].astype(F32), sl, use_rope) * k_scale
            kr_ref[sl, :] = k.astype(BF16)
            kk = jnp.concatenate([k * k_dec_f, k * k_dec_b], axis=1).astype(BF16)
            kv_ref[t] = _dot_tn(kk, vr[0, sl, :])
            return carry
        lax.fori_loop(0, nc, kv_body, 0)

        def fwd_body(t, s):
            st_ref[t, 0:dk, :] = s.astype(BF16)
            return s * dec_f + kv_ref[t, 0:dk, :]
        sf = lax.fori_loop(0, nc, fwd_body, s0f)

        def bwd_body(t, s):
            n = nc - 1 - t
            st_ref[n, dk:2 * dk, :] = s.astype(BF16)
            return s * dec_b + kv_ref[n, dk:2 * dk, :]
        sb = lax.fori_loop(0, nc, bwd_body, s0b)

        if outr is not None:
            def out_body(t, carry):
                sl = chunk(t)
                q = rope(qr[0, sl, :].astype(F32), sl, use_rope)
                a = _dot_nt(q.astype(BF16), kr_ref[sl, :])
                qq = jnp.concatenate([q * q_dec_f, q * q_dec_b], axis=1).astype(BF16)
                o = _dot((a * intra).astype(BF16), vr[0, sl, :]) + _dot(qq, st_ref[t])
                o = o * lax.rsqrt(jnp.mean(o * o, axis=-1, keepdims=True) + NORM_EPS)
                outr[0, sl, :] = o.astype(outr.dtype)
                return carry
            lax.fori_loop(0, nc, out_body, 0)
        return sf, sb

    zero = jnp.zeros((dk, RET_VAL_DIM), F32)
    sf, sb = run(qc_ref, kc_ref, vc_ref, oc_ref, kc_ref.shape[1], False, zero, zero)
    run(q_ref, k_ref, v_ref, o_ref, k_ref.shape[1], True, sf, sb)


def _retention(decay_logit, cosf, sinf, u3, uc3, *, ctx_k_off, ctx_v_off, ctx_q_off):
    b, s, _ = u3.shape
    lc = uc3.shape[1]
    with_ctx_q = ctx_q_off is not None
    kb = lambda off: off // RET_KEY_DIM
    vb = lambda off: off // RET_VAL_DIM
    in_specs = [
        pl.BlockSpec(decay_logit.shape, lambda bb, h: (0, 0)),
        pl.BlockSpec(cosf.shape, lambda bb, h: (0, 0)),
        pl.BlockSpec(sinf.shape, lambda bb, h: (0, 0)),
        pl.BlockSpec((1, s, RET_KEY_DIM), lambda bb, h: (bb, 0, kb(OFF_RET_Q) + h)),
        pl.BlockSpec((1, s, RET_KEY_DIM), lambda bb, h: (bb, 0, kb(OFF_RET_K) + h)),
        pl.BlockSpec((1, s, RET_VAL_DIM), lambda bb, h: (bb, 0, vb(OFF_RET_V) + h)),
        pl.BlockSpec((1, lc, RET_KEY_DIM), lambda bb, h: (bb, 0, kb(ctx_k_off) + h)),
        pl.BlockSpec((1, lc, RET_VAL_DIM), lambda bb, h: (bb, 0, vb(ctx_v_off) + h)),
    ]
    args = [decay_logit, cosf, sinf, u3, u3, u3, uc3, uc3]
    out_specs = [pl.BlockSpec((1, s, RET_VAL_DIM), lambda bb, h: (bb, 0, h))]
    out_shape = [jax.ShapeDtypeStruct((b, s, N_HEADS * RET_VAL_DIM), BF16)]
    if with_ctx_q:
        in_specs.append(pl.BlockSpec((1, lc, RET_KEY_DIM), lambda bb, h: (bb, 0, kb(ctx_q_off) + h)))
        args.append(uc3)
        out_specs.append(pl.BlockSpec((1, lc, RET_VAL_DIM), lambda bb, h: (bb, 0, h)))
        out_shape.append(jax.ShapeDtypeStruct((b, lc, N_HEADS * RET_VAL_DIM), BF16))
    nc = s // RET_CHUNK
    outs = pl.pallas_call(
        functools.partial(_ret_kernel, with_ctx_q=with_ctx_q),
        grid=(b, N_HEADS),
        in_specs=in_specs,
        out_specs=out_specs,
        out_shape=out_shape,
        scratch_shapes=[
            pltpu.VMEM((nc, 2 * RET_KEY_DIM, RET_VAL_DIM), F32),
            pltpu.VMEM((nc, 2 * RET_KEY_DIM, RET_VAL_DIM), BF16),
            pltpu.VMEM((s, RET_KEY_DIM), BF16),
        ],
        compiler_params=pltpu.CompilerParams(
            dimension_semantics=("arbitrary", "arbitrary"), vmem_limit_bytes=VMEM_LIMIT),
        name="retention",
    )(*args)
    return outs if with_ctx_q else (outs[0], None)


def _merge_kernel(*refs, final):
    if final:
        (ona_ref, zna_ref, oret_ref, zret_ref, gna_ref, gret_ref, x_ref, gate_ref,
         wna_ref, wret_ref, wout_ref, fg_ref, o_ref) = refs
    else:
        (ona_ref, zna_ref, oret_ref, zret_ref, gna_ref, gret_ref, x_ref, gate_ref,
         wna_ref, wret_ref, wout_ref, o_ref) = refs
    zna = zna_ref[...].astype(F32)
    a = (ona_ref[...].astype(F32) * (zna * _sigmoid(zna))).astype(BF16)
    y_na = _dot(a, wna_ref[...])
    zret = zret_ref[...].astype(F32)
    r = (oret_ref[...].astype(F32) * (zret * _sigmoid(zret))).astype(BF16)
    y_ret = _dot(r, wret_ref[...])
    merged = (_sigmoid(gna_ref[...].astype(F32)) * y_na
              + _sigmoid(gret_ref[...].astype(F32)) * y_ret).astype(BF16)
    xn = x_ref[...] + gate_ref[0] * _dot(merged, wout_ref[...])
    if final:
        ms = jnp.mean(xn * xn, axis=-1, keepdims=True)
        xn = xn * lax.rsqrt(ms + NORM_EPS) * fg_ref[...]
    o_ref[...] = xn


def _merge(o_na, o_ret, u2, x2, gate, wna, wret, wout, final_g, *, rows_per_mod, mod_row, tm):
    m, d = x2.shape
    if mod_row is None:
        gate_idx = lambda i: ((i * tm) // rows_per_mod, 0, 0)
    else:
        gate_idx = lambda i: (mod_row, 0, 0)
    resident = lambda shape: pl.BlockSpec(shape, lambda i: (0, 0), pipeline_mode=pl.Buffered(1))
    w_na_cols = N_HEADS * NA_DIM
    w_ret_cols = N_HEADS * RET_VAL_DIM
    in_specs = [
        pl.BlockSpec((tm, w_na_cols), lambda i: (i, 0)),
        pl.BlockSpec((tm, w_na_cols), lambda i: (i, OFF_NA_Z // w_na_cols)),
        pl.BlockSpec((tm, w_ret_cols), lambda i: (i, 0)),
        pl.BlockSpec((tm, w_ret_cols), lambda i: (i, OFF_RET_Z // w_ret_cols)),
        pl.BlockSpec((tm, d), lambda i: (i, OFF_G_NA // d)),
        pl.BlockSpec((tm, d), lambda i: (i, OFF_G_RET // d)),
        pl.BlockSpec((tm, d), lambda i: (i, 0)),
        pl.BlockSpec((1, 1, d), gate_idx),
        resident(wna.shape),
        resident(wret.shape),
        resident(wout.shape),
    ]
    args = [o_na, u2, o_ret, u2, u2, u2, x2, gate, wna, wret, wout]
    final = final_g is not None
    if final:
        in_specs.append(pl.BlockSpec((1, d), lambda i: (0, 0)))
        args.append(final_g)
    return pl.pallas_call(
        functools.partial(_merge_kernel, final=final),
        grid=(m // tm,),
        in_specs=in_specs,
        out_specs=pl.BlockSpec((tm, d), lambda i: (i, 0)),
        out_shape=jax.ShapeDtypeStruct((m, d), F32),
        compiler_params=pltpu.CompilerParams(
            dimension_semantics=("arbitrary",), vmem_limit_bytes=VMEM_LIMIT),
        name="merge",
    )(*args)


def _rope_tables(n_tokens):
    t = jnp.arange(n_tokens)
    row = (t // GRID_W).astype(F32)
    col = (t % GRID_W).astype(F32)
    n_freq = RET_KEY_DIM // 4
    inv_freq = ROPE_BASE ** (-jnp.arange(n_freq, dtype=F32) / n_freq)
    ang = jnp.concatenate([row[:, None] * inv_freq, col[:, None] * inv_freq], axis=-1)
    cos, sin = jnp.cos(ang), jnp.sin(ang)
    return jnp.concatenate([cos, cos], axis=-1), jnp.concatenate([-sin, sin], axis=-1)


def kernel(x, c, ctx, c_ctx, ada_w, ada_b, norm_g, w_in, na_rpb, ret_decay_logit,
           w_proj_na, w_proj_ret, w_out, final_g):
    b, s, d = x.shape
    lc = ctx.shape[1]
    depth = ada_w.shape[0]

    c_all = jnp.concatenate([c, c_ctx[None], jnp.zeros((MOD_ROWS - b - 1, d), F32)], axis=0)
    mod = _modulation(c_all, ada_w, ada_b)
    cosf, sinf = _rope_tables(s)

    x_lat = x.reshape(b * s, d)
    x_ctx = ctx.reshape(b * lc, d)
    for l in range(depth):
        last = l == depth - 1
        shift = mod[l, :, 0:d].reshape(MOD_ROWS, 1, d)
        scale = mod[l, :, d:2 * d].reshape(MOD_ROWS, 1, d)
        gate = mod[l, :, 2 * d:3 * d].reshape(MOD_ROWS, 1, d)
        g = norm_g[l].reshape(1, d)
        w = w_in[l].astype(BF16)

        u = _inproj(x_lat, shift, scale, g, w, rows_per_mod=s, mod_row=None,
                    col_map=lambda j: j, n_out=IN_COLS, tm=1024, tn=512)
        if last:
            nb = 1024 // 512
            uc = _inproj(x_ctx, shift, scale, g, w, rows_per_mod=lc, mod_row=CTX_MOD_ROW,
                         col_map=lambda j: jnp.where(j < 2 * nb, j + nb, j + 3 * nb),
                         n_out=CTX_KV_COLS, tm=1024, tn=512)
            offs = dict(na_k=COFF_NA_K, na_v=COFF_NA_V, ret_k=COFF_RET_K, ret_v=COFF_RET_V)
            na_q = ret_q = None
        else:
            uc = _inproj(x_ctx, shift, scale, g, w, rows_per_mod=lc, mod_row=CTX_MOD_ROW,
                         col_map=lambda j: j, n_out=IN_COLS, tm=1024, tn=512)
            offs = dict(na_k=OFF_NA_K, na_v=OFF_NA_V, ret_k=OFF_RET_K, ret_v=OFF_RET_V)
            na_q, ret_q = OFF_NA_Q, OFF_RET_Q
        u3 = u.reshape(b, s, IN_COLS)
        uc3 = uc.reshape(b, lc, uc.shape[-1])

        o_na, o_na_ctx = _na_attention(_na_pair_bias(na_rpb[l]), u3, uc3,
                                       ctx_k_off=offs['na_k'], ctx_v_off=offs['na_v'], ctx_q_off=na_q)
        o_ret, o_ret_ctx = _retention(ret_decay_logit[l], cosf, sinf, u3, uc3,
                                      ctx_k_off=offs['ret_k'], ctx_v_off=offs['ret_v'], ctx_q_off=ret_q)

        wna = w_proj_na[l].astype(BF16)
        wret = w_proj_ret[l].astype(BF16)
        wout = w_out[l].astype(BF16)
        x_lat = _merge(o_na.reshape(b * s, -1), o_ret.reshape(b * s, -1), u, x_lat, gate,
                       wna, wret, wout, final_g.reshape(1, d) if last else None,
                       rows_per_mod=s, mod_row=None, tm=256)
        if not last:
            x_ctx = _merge(o_na_ctx.reshape(b * lc, -1), o_ret_ctx.reshape(b * lc, -1), uc, x_ctx, gate,
                           wna, wret, wout, None, rows_per_mod=lc, mod_row=CTX_MOD_ROW, tm=256)
    return x_lat.reshape(b, s, d)
```

```python
import functools

import jax
import jax.numpy as jnp
from jax import lax
from jax.experimental import pallas as pl
from jax.experimental.pallas import tpu as pltpu

F32 = jnp.float32
BF16 = jnp.bfloat16

D_MODEL = 2048
GRID_W = 64
N_HEADS = 8
NA_DIM = 128
NA_WIN_ROWS = 8
NA_WIN_COLS = 16
RET_KEY_DIM = 128
RET_VAL_DIM = 256
RET_CHUNK = 128
ROPE_BASE = 10000.0
NORM_EPS = 1e-6
MASK_VALUE = -1e30
LOG2E = 1.4426950408889634

OFF_NA_Q, OFF_NA_K, OFF_NA_V, OFF_NA_Z = 0, 1024, 2048, 3072
OFF_RET_Q, OFF_RET_K, OFF_RET_V, OFF_RET_Z = 4096, 5120, 6144, 8192
OFF_G_NA, OFF_G_RET = 10240, 12288
IN_COLS = 14336
CTX_KV_COLS = 5120
COFF_NA_K, COFF_NA_V, COFF_RET_K, COFF_RET_V = 0, 1024, 2048, 3072

MOD_ROWS = 16
CTX_MOD_ROW = 8

NA_QBLK = 4 * GRID_W
NA_KBLK = 12 * GRID_W

VMEM_LIMIT = 56 * 1024 * 1024
INPROJ_TM = 1024
INPROJ_TN = 2048
INPROJ_CTX_TN = 1024
MERGE_TM = 256
RET_UNROLL = 16
NA_UNROLL = 2


def _dot(a, b):
    return jnp.dot(a, b, preferred_element_type=F32)


def _dot_nt(a, b):
    return lax.dot_general(a, b, (((1,), (1,)), ((), ())), preferred_element_type=F32)


def _dot_tn(a, b):
    return lax.dot_general(a, b, (((0,), (0,)), ((), ())), preferred_element_type=F32)


def _sigmoid(x):
    return 1.0 / (1.0 + jnp.exp(-x))


def _mod_kernel(c_ref, w_ref, b_ref, o_ref):
    c = c_ref[...]
    cs = (c * _sigmoid(c)).astype(BF16)
    o_ref[0] = _dot(cs, w_ref[0].astype(BF16)) + b_ref[0]


def _modulation(c_all, ada_w, ada_b, tn=512):
    depth, d, n = ada_w.shape
    return pl.pallas_call(
        _mod_kernel,
        grid=(depth, n // tn),
        in_specs=[
            pl.BlockSpec((MOD_ROWS, d), lambda l, j: (0, 0)),
            pl.BlockSpec((1, d, tn), lambda l, j: (l, 0, j)),
            pl.BlockSpec((1, 1, tn), lambda l, j: (l, 0, j)),
        ],
        out_specs=pl.BlockSpec((1, MOD_ROWS, tn), lambda l, j: (l, 0, j)),
        out_shape=jax.ShapeDtypeStruct((depth, MOD_ROWS, n), F32),
        compiler_params=pltpu.CompilerParams(
            dimension_semantics=("arbitrary", "arbitrary"), vmem_limit_bytes=VMEM_LIMIT),
        name="adaln_mod",
    )(c_all, ada_w, ada_b.reshape(depth, 1, n))


def _inproj_kernel(x_ref, shift_ref, scale_ref, g_ref, w_ref, cs_ref, o_ref, h_ref, *, row_chunk):
    @pl.when(pl.program_id(1) == 0)
    def _():
        def body(r, carry):
            sl = pl.ds(pl.multiple_of(r * row_chunk, row_chunk), row_chunk)
            x = x_ref[sl, :]
            ms = jnp.mean(x * x, axis=-1, keepdims=True)
            y = x * lax.rsqrt(ms + NORM_EPS) * g_ref[...]
            h = y * (1.0 + scale_ref[0]) + shift_ref[0]
            h_ref[sl, :] = h.astype(BF16)
            return carry
        lax.fori_loop(0, x_ref.shape[0] // row_chunk, body, 0)

    o_ref[...] = (_dot(h_ref[...], w_ref[...]) * cs_ref[...]).astype(o_ref.dtype)


def _inproj(x2, shift, scale, g, w, col_scale, *, rows_per_mod, mod_row, col_map, n_out, tm, tn):
    m, d = x2.shape
    if mod_row is None:
        mod_idx = lambda i, j: ((i * tm) // rows_per_mod, 0, 0)
    else:
        mod_idx = lambda i, j: (mod_row, 0, 0)
    return pl.pallas_call(
        functools.partial(_inproj_kernel, row_chunk=128),
        grid=(m // tm, n_out // tn),
        in_specs=[
            pl.BlockSpec((tm, d), lambda i, j: (i, 0)),
            pl.BlockSpec((1, 1, d), mod_idx),
            pl.BlockSpec((1, 1, d), mod_idx),
            pl.BlockSpec((1, d), lambda i, j: (0, 0)),
            pl.BlockSpec((d, tn), lambda i, j: (0, col_map(j))),
            pl.BlockSpec((1, tn), lambda i, j: (0, col_map(j))),
        ],
        out_specs=pl.BlockSpec((tm, tn), lambda i, j: (i, j)),
        out_shape=jax.ShapeDtypeStruct((m, n_out), BF16),
        scratch_shapes=[pltpu.VMEM((tm, d), BF16)],
        compiler_params=pltpu.CompilerParams(
            dimension_semantics=("arbitrary", "arbitrary"), vmem_limit_bytes=VMEM_LIMIT),
        name="inproj",
    )(x2, shift, scale, g, w, col_scale)


def _na_window_row0(r, rows):
    return min(max(r - NA_WIN_ROWS // 2, 0), rows - NA_WIN_ROWS)


def _na_key_row0(i, rows):
    return min(max(4 * i - 4, 0), rows - 12)


def _na_kernel(*refs, rows, with_ctx_q):
    if with_ctx_q:
        (pb_ref, q_ref, k_ref, v_ref, kc_ref, vc_ref, qc_ref, o_ref, oc_ref,
         bias_ref, vx_ref, vcx_ref) = refs
    else:
        pb_ref, q_ref, k_ref, v_ref, kc_ref, vc_ref, o_ref, bias_ref, vx_ref, vcx_ref = refs
    n_blk = rows // 4

    @pl.when(pl.program_id(1) == 0)
    def _():
        lane = lax.broadcasted_iota(jnp.int32, (GRID_W, 2 * GRID_W), 1)
        neg = jnp.full((GRID_W, 2 * GRID_W), MASK_VALUE, F32)
        for ti, i in enumerate((0, 1, n_blk - 1)):
            krow0 = _na_key_row0(i, rows)
            for rl in range(4):
                r = 4 * i + rl
                r0 = _na_window_row0(r, rows)
                for j in range(6):
                    ka = krow0 + 2 * j
                    in_a = r0 <= ka < r0 + NA_WIN_ROWS
                    in_b = r0 <= ka + 1 < r0 + NA_WIN_ROWS
                    if in_a or in_b:
                        t = pb_ref[0, ka - r + NA_WIN_ROWS]
                        if not in_b:
                            t = jnp.where(lane < GRID_W, t, neg)
                        if not in_a:
                            t = jnp.where(lane >= GRID_W, t, neg)
                    else:
                        t = neg
                    bias_ref[ti, rl * GRID_W:(rl + 1) * GRID_W, j * 128:(j + 1) * 128] = t

    vx_ref[:, 0:NA_DIM] = v_ref[0]
    vx_ref[:, NA_DIM:] = jnp.ones((vx_ref.shape[0], NA_DIM), BF16)
    vcx_ref[:, 0:NA_DIM] = vc_ref[0]
    vcx_ref[:, NA_DIM:] = jnp.ones((vcx_ref.shape[0], NA_DIM), BF16)
    kc = kc_ref[0]

    def body(i, carry):
        qsl = pl.ds(pl.multiple_of(i * NA_QBLK, NA_QBLK), NA_QBLK)
        kstart = pl.multiple_of(jnp.clip(i - 1, 0, n_blk - 3) * NA_QBLK, NA_QBLK)
        ksl = pl.ds(kstart, NA_KBLK)
        ti = jnp.where(i == 0, 0, jnp.where(i == n_blk - 1, 2, 1))
        q = q_ref[0, qsl, :]
        s = _dot_nt(q, k_ref[0, ksl, :]) + bias_ref[ti]
        sc = _dot_nt(q, kc)
        m = jnp.maximum(jnp.max(s, axis=-1, keepdims=True), jnp.max(sc, axis=-1, keepdims=True))
        p = jnp.exp2(s - m).astype(BF16)
        pc = jnp.exp2(sc - m).astype(BF16)
        half = NA_QBLK // 2
        for r0 in (0, half):
            ox = _dot(p[r0:r0 + half], vx_ref[ksl, :]) + _dot(pc[r0:r0 + half], vcx_ref[...])
            o_ref[0, pl.ds(pl.multiple_of(i * NA_QBLK + r0, half), half), :] = (
                ox[:, 0:NA_DIM] / ox[:, NA_DIM:]).astype(o_ref.dtype)
        return carry

    lax.fori_loop(0, n_blk, body, 0, unroll=NA_UNROLL)

    if with_ctx_q:
        sc = _dot_nt(qc_ref[0], kc)
        pc = jnp.exp2(sc - jnp.max(sc, axis=-1, keepdims=True)).astype(BF16)
        ox = _dot(pc, vcx_ref[...])
        oc_ref[0] = (ox[:, 0:NA_DIM] / ox[:, NA_DIM:]).astype(oc_ref.dtype)


def _na_attention(pb, u3, uc3, *, ctx_k_off, ctx_v_off, ctx_q_off):
    b, s, _ = u3.shape
    lc = uc3.shape[1]
    rows = s // GRID_W
    with_ctx_q = ctx_q_off is not None
    hb = lambda off: off // NA_DIM
    in_specs = [
        pl.BlockSpec((1,) + pb.shape[1:], lambda h, bb: (h, 0, 0, 0)),
        pl.BlockSpec((1, s, NA_DIM), lambda h, bb: (bb, 0, hb(OFF_NA_Q) + h)),
        pl.BlockSpec((1, s, NA_DIM), lambda h, bb: (bb, 0, hb(OFF_NA_K) + h)),
        pl.BlockSpec((1, s, NA_DIM), lambda h, bb: (bb, 0, hb(OFF_NA_V) + h)),
        pl.BlockSpec((1, lc, NA_DIM), lambda h, bb: (bb, 0, hb(ctx_k_off) + h)),
        pl.BlockSpec((1, lc, NA_DIM), lambda h, bb: (bb, 0, hb(ctx_v_off) + h)),
    ]
    args = [pb, u3, u3, u3, uc3, uc3]
    out_specs = [pl.BlockSpec((1, s, NA_DIM), lambda h, bb: (bb, 0, h))]
    out_shape = [jax.ShapeDtypeStruct((b, s, N_HEADS * NA_DIM), BF16)]
    if with_ctx_q:
        in_specs.append(pl.BlockSpec((1, lc, NA_DIM), lambda h, bb: (bb, 0, hb(ctx_q_off) + h)))
        args.append(uc3)
        out_specs.append(pl.BlockSpec((1, lc, NA_DIM), lambda h, bb: (bb, 0, h)))
        out_shape.append(jax.ShapeDtypeStruct((b, lc, N_HEADS * NA_DIM), BF16))
    outs = pl.pallas_call(
        functools.partial(_na_kernel, rows=rows, with_ctx_q=with_ctx_q),
        grid=(N_HEADS, b),
        in_specs=in_specs,
        out_specs=out_specs,
        out_shape=out_shape,
        scratch_shapes=[
            pltpu.VMEM((3, NA_QBLK, NA_KBLK), F32),
            pltpu.VMEM((s, 2 * NA_DIM), BF16),
            pltpu.VMEM((lc, 2 * NA_DIM), BF16),
        ],
        compiler_params=pltpu.CompilerParams(
            dimension_semantics=("arbitrary", "arbitrary"), vmem_limit_bytes=VMEM_LIMIT),
        name="na_attention",
    )(*args)
    return outs if with_ctx_q else (outs[0], None)


def _na_pair_bias(rpb):
    cq = jnp.arange(GRID_W)[:, None]
    ck = jnp.arange(GRID_W)[None, :]
    c0 = jnp.clip(cq - NA_WIN_COLS // 2, 0, GRID_W - NA_WIN_COLS)
    col_in = (ck >= c0) & (ck < c0 + NA_WIN_COLS)
    dc = jnp.clip(ck - cq + (NA_WIN_COLS - 1), 0, 2 * NA_WIN_COLS - 2)
    onehot = ((dc[None] == jnp.arange(2 * NA_WIN_COLS - 1)[:, None, None]) & col_in[None]).astype(F32)
    cb = jnp.einsum('hrd,dqk->hrqk', rpb.astype(F32), onehot, precision=lax.Precision.HIGHEST)
    cb = jnp.where(col_in[None, None], cb * LOG2E, MASK_VALUE)
    neg = jnp.full_like(cb[:, :1], MASK_VALUE)
    cbx = jnp.concatenate([neg, cb, neg], axis=1)
    return jnp.concatenate([cbx[:, :-1], cbx[:, 1:]], axis=-1)


def _ret_kernel(*refs, with_ctx_q):
    if with_ctx_q:
        (dl_ref, cos_ref, sin_ref, q_ref, k_ref, v_ref, kc_ref, vc_ref, qc_ref,
         o_ref, oc_ref, kv_ref, st_ref, kr_ref) = refs
    else:
        (dl_ref, cos_ref, sin_ref, q_ref, k_ref, v_ref, kc_ref, vc_ref,
         o_ref, kv_ref, st_ref, kr_ref) = refs
        qc_ref = oc_ref = None
    cs = RET_CHUNK
    dk = RET_KEY_DIM

    dl = dl_ref[...]
    lg = -(jnp.maximum(-dl, 0.0) + jnp.log1p(jnp.exp(-jnp.abs(dl))))
    head = lax.broadcasted_iota(jnp.int32, dl.shape, 1)
    lgh = jnp.sum(jnp.where(head == pl.program_id(1), lg, 0.0), axis=1, keepdims=True)
    lgf = lgh[0:1, :]
    lgb = lgh[1:2, :]

    pos = lax.broadcasted_iota(jnp.int32, (cs, dk), 0).astype(F32)
    q_dec_f = jnp.exp((pos + 1.0) * lgf)
    q_dec_b = jnp.exp((cs - pos) * lgb)
    k_dec_f = jnp.exp((cs - 1.0 - pos) * lgf)
    k_dec_b = jnp.exp(pos * lgb)
    dec_f = jnp.exp(cs * lgf)
    dec_b = jnp.exp(cs * lgb)
    ii = lax.broadcasted_iota(jnp.int32, (cs, cs), 0)
    jj = lax.broadcasted_iota(jnp.int32, (cs, cs), 1)
    diff = (ii - jj).astype(F32)
    intra = jnp.where(diff > 0, jnp.exp(jnp.maximum(diff, 0.0) * lgf),
                      jnp.where(diff < 0, jnp.exp(jnp.maximum(-diff, 0.0) * lgb), 2.0))

    def rope(t, sl, use_rope):
        if not use_rope:
            return t
        return t * cos_ref[sl, :] + pltpu.roll(t, dk // 2, 1) * sin_ref[sl, :]

    def run(qr, kr, vr, outr, n_tok, use_rope, s0f, s0b):
        nc = n_tok // cs

        def chunk(t):
            return pl.ds(pl.multiple_of(t * cs, cs), cs)

        def kv_body(t, carry):
            sl = chunk(t)
            k = rope(kr[0, sl, :].astype(F32), sl, use_rope)
            kr_ref[sl, :] = k.astype(BF16)
            kk = jnp.concatenate([k * k_dec_f, k * k_dec_b], axis=1).astype(BF16)
            kv_ref[t] = _dot_tn(kk, vr[0, sl, :])
            return carry
        lax.fori_loop(0, nc, kv_body, 0, unroll=min(nc, RET_UNROLL))

        def fwd_body(t, s):
            st_ref[t, 0:dk, :] = s.astype(BF16)
            return s * dec_f + kv_ref[t, 0:dk, :]
        sf = lax.fori_loop(0, nc, fwd_body, s0f)

        def bwd_body(t, s):
            n = nc - 1 - t
            st_ref[n, dk:2 * dk, :] = s.astype(BF16)
            return s * dec_b + kv_ref[n, dk:2 * dk, :]
        sb = lax.fori_loop(0, nc, bwd_body, s0b)

        if outr is not None:
            def out_body(t, carry):
                sl = chunk(t)
                q = rope(qr[0, sl, :].astype(F32), sl, use_rope)
                a = _dot_nt(q.astype(BF16), kr_ref[sl, :])
                qq = jnp.concatenate([q * q_dec_f, q * q_dec_b], axis=1).astype(BF16)
                o = _dot((a * intra).astype(BF16), vr[0, sl, :]) + _dot(qq, st_ref[t])
                o = o * lax.rsqrt(jnp.mean(o * o, axis=-1, keepdims=True) + NORM_EPS)
                outr[0, sl, :] = o.astype(outr.dtype)
                return carry
            lax.fori_loop(0, nc, out_body, 0, unroll=min(nc, RET_UNROLL))
        return sf, sb

    zero = jnp.zeros((dk, RET_VAL_DIM), F32)
    sf, sb = run(qc_ref, kc_ref, vc_ref, oc_ref, kc_ref.shape[1], False, zero, zero)
    run(q_ref, k_ref, v_ref, o_ref, k_ref.shape[1], True, sf, sb)


def _retention(decay_logit, cosf, sinf, u3, uc3, *, ctx_k_off, ctx_v_off, ctx_q_off):
    b, s, _ = u3.shape
    lc = uc3.shape[1]
    with_ctx_q = ctx_q_off is not None
    kb = lambda off: off // RET_KEY_DIM
    vb = lambda off: off // RET_VAL_DIM
    in_specs = [
        pl.BlockSpec(decay_logit.shape, lambda bb, h: (0, 0)),
        pl.BlockSpec(cosf.shape, lambda bb, h: (0, 0)),
        pl.BlockSpec(sinf.shape, lambda bb, h: (0, 0)),
        pl.BlockSpec((1, s, RET_KEY_DIM), lambda bb, h: (bb, 0, kb(OFF_RET_Q) + h)),
        pl.BlockSpec((1, s, RET_KEY_DIM), lambda bb, h: (bb, 0, kb(OFF_RET_K) + h)),
        pl.BlockSpec((1, s, RET_VAL_DIM), lambda bb, h: (bb, 0, vb(OFF_RET_V) + h)),
        pl.BlockSpec((1, lc, RET_KEY_DIM), lambda bb, h: (bb, 0, kb(ctx_k_off) + h)),
        pl.BlockSpec((1, lc, RET_VAL_DIM), lambda bb, h: (bb, 0, vb(ctx_v_off) + h)),
    ]
    args = [decay_logit, cosf, sinf, u3, u3, u3, uc3, uc3]
    out_specs = [pl.BlockSpec((1, s, RET_VAL_DIM), lambda bb, h: (bb, 0, h))]
    out_shape = [jax.ShapeDtypeStruct((b, s, N_HEADS * RET_VAL_DIM), BF16)]
    if with_ctx_q:
        in_specs.append(pl.BlockSpec((1, lc, RET_KEY_DIM), lambda bb, h: (bb, 0, kb(ctx_q_off) + h)))
        args.append(uc3)
        out_specs.append(pl.BlockSpec((1, lc, RET_VAL_DIM), lambda bb, h: (bb, 0, h)))
        out_shape.append(jax.ShapeDtypeStruct((b, lc, N_HEADS * RET_VAL_DIM), BF16))
    nc = s // RET_CHUNK
    outs = pl.pallas_call(
        functools.partial(_ret_kernel, with_ctx_q=with_ctx_q),
        grid=(b, N_HEADS),
        in_specs=in_specs,
        out_specs=out_specs,
        out_shape=out_shape,
        scratch_shapes=[
            pltpu.VMEM((nc, 2 * RET_KEY_DIM, RET_VAL_DIM), F32),
            pltpu.VMEM((nc, 2 * RET_KEY_DIM, RET_VAL_DIM), BF16),
            pltpu.VMEM((s, RET_KEY_DIM), BF16),
        ],
        compiler_params=pltpu.CompilerParams(
            dimension_semantics=("arbitrary", "arbitrary"), vmem_limit_bytes=VMEM_LIMIT),
        name="retention",
    )(*args)
    return outs if with_ctx_q else (outs[0], None)


def _merge_kernel(*refs, final):
    if final:
        (ona_ref, zna_ref, oret_ref, zret_ref, gna_ref, gret_ref, x_ref, gate_ref,
         wna_ref, wret_ref, wout_ref, fg_ref, o_ref) = refs
    else:
        (ona_ref, zna_ref, oret_ref, zret_ref, gna_ref, gret_ref, x_ref, gate_ref,
         wna_ref, wret_ref, wout_ref, o_ref) = refs
    zna = zna_ref[...].astype(F32)
    a = (ona_ref[...].astype(F32) * (zna * _sigmoid(zna))).astype(BF16)
    y_na = _dot(a, wna_ref[...])
    zret = zret_ref[...].astype(F32)
    r = (oret_ref[...].astype(F32) * (zret * _sigmoid(zret))).astype(BF16)
    y_ret = _dot(r, wret_ref[...])
    merged = (_sigmoid(gna_ref[...].astype(F32)) * y_na
              + _sigmoid(gret_ref[...].astype(F32)) * y_ret).astype(BF16)
    xn = x_ref[...] + gate_ref[0] * _dot(merged, wout_ref[...])
    if final:
        ms = jnp.mean(xn * xn, axis=-1, keepdims=True)
        xn = xn * lax.rsqrt(ms + NORM_EPS) * fg_ref[...]
    o_ref[...] = xn


def _merge(o_na, o_ret, u2, x2, gate, wna, wret, wout, final_g, *, rows_per_mod, mod_row, tm):
    m, d = x2.shape
    if mod_row is None:
        gate_idx = lambda i: ((i * tm) // rows_per_mod, 0, 0)
    else:
        gate_idx = lambda i: (mod_row, 0, 0)
    resident = lambda shape: pl.BlockSpec(shape, lambda i: (0, 0), pipeline_mode=pl.Buffered(1))
    w_na_cols = N_HEADS * NA_DIM
    w_ret_cols = N_HEADS * RET_VAL_DIM
    in_specs = [
        pl.BlockSpec((tm, w_na_cols), lambda i: (i, 0)),
        pl.BlockSpec((tm, w_na_cols), lambda i: (i, OFF_NA_Z // w_na_cols)),
        pl.BlockSpec((tm, w_ret_cols), lambda i: (i, 0)),
        pl.BlockSpec((tm, w_ret_cols), lambda i: (i, OFF_RET_Z // w_ret_cols)),
        pl.BlockSpec((tm, d), lambda i: (i, OFF_G_NA // d)),
        pl.BlockSpec((tm, d), lambda i: (i, OFF_G_RET // d)),
        pl.BlockSpec((tm, d), lambda i: (i, 0)),
        pl.BlockSpec((1, 1, d), gate_idx),
        resident(wna.shape),
        resident(wret.shape),
        resident(wout.shape),
    ]
    args = [o_na, u2, o_ret, u2, u2, u2, x2, gate, wna, wret, wout]
    final = final_g is not None
    if final:
        in_specs.append(pl.BlockSpec((1, d), lambda i: (0, 0)))
        args.append(final_g)
    return pl.pallas_call(
        functools.partial(_merge_kernel, final=final),
        grid=(m // tm,),
        in_specs=in_specs,
        out_specs=pl.BlockSpec((tm, d), lambda i: (i, 0)),
        out_shape=jax.ShapeDtypeStruct((m, d), F32),
        compiler_params=pltpu.CompilerParams(
            dimension_semantics=("arbitrary",), vmem_limit_bytes=VMEM_LIMIT),
        name="merge",
    )(*args)


def _rope_tables(n_tokens):
    t = jnp.arange(n_tokens)
    row = (t // GRID_W).astype(F32)
    col = (t % GRID_W).astype(F32)
    n_freq = RET_KEY_DIM // 4
    inv_freq = ROPE_BASE ** (-jnp.arange(n_freq, dtype=F32) / n_freq)
    ang = jnp.concatenate([row[:, None] * inv_freq, col[:, None] * inv_freq], axis=-1)
    cos, sin = jnp.cos(ang), jnp.sin(ang)
    return jnp.concatenate([cos, cos], axis=-1), jnp.concatenate([-sin, sin], axis=-1)


def kernel(x, c, ctx, c_ctx, ada_w, ada_b, norm_g, w_in, na_rpb, ret_decay_logit,
           w_proj_na, w_proj_ret, w_out, final_g):
    b, s, d = x.shape
    lc = ctx.shape[1]
    depth = ada_w.shape[0]

    c_all = jnp.concatenate([c, c_ctx[None], jnp.zeros((MOD_ROWS - b - 1, d), F32)], axis=0)
    mod = _modulation(c_all, ada_w, ada_b)
    cosf, sinf = _rope_tables(s)
    cols = jnp.arange(IN_COLS)
    col_scale = jnp.where((cols >= OFF_NA_Q) & (cols < OFF_NA_K), NA_DIM ** -0.5 * LOG2E,
                          jnp.where((cols >= OFF_RET_K) & (cols < OFF_RET_V), RET_KEY_DIM ** -0.5, 1.0)
                          ).astype(F32).reshape(1, IN_COLS)

    x_lat = x.reshape(b * s, d)
    x_ctx = ctx.reshape(b * lc, d)
    for l in range(depth):
        last = l == depth - 1
        shift = mod[l, :, 0:d].reshape(MOD_ROWS, 1, d)
        scale = mod[l, :, d:2 * d].reshape(MOD_ROWS, 1, d)
        gate = mod[l, :, 2 * d:3 * d].reshape(MOD_ROWS, 1, d)
        g = norm_g[l].reshape(1, d)
        w = w_in[l].astype(BF16)

        u = _inproj(x_lat, shift, scale, g, w, col_scale, rows_per_mod=s, mod_row=None,
                    col_map=lambda j: j, n_out=IN_COLS, tm=INPROJ_TM, tn=INPROJ_TN)
        if last:
            uc = _inproj(x_ctx, shift, scale, g, w, col_scale, rows_per_mod=lc, mod_row=CTX_MOD_ROW,
                         col_map=lambda j: jnp.where(j < 2, j + 1, j + 3),
                         n_out=CTX_KV_COLS, tm=INPROJ_TM, tn=INPROJ_CTX_TN)
            offs = dict(na_k=COFF_NA_K, na_v=COFF_NA_V, ret_k=COFF_RET_K, ret_v=COFF_RET_V)
            na_q = ret_q = None
        else:
            uc = _inproj(x_ctx, shift, scale, g, w, col_scale, rows_per_mod=lc, mod_row=CTX_MOD_ROW,
                         col_map=lambda j: j, n_out=IN_COLS, tm=INPROJ_TM, tn=INPROJ_CTX_TN)
            offs = dict(na_k=OFF_NA_K, na_v=OFF_NA_V, ret_k=OFF_RET_K, ret_v=OFF_RET_V)
            na_q, ret_q = OFF_NA_Q, OFF_RET_Q
        u3 = u.reshape(b, s, IN_COLS)
        uc3 = uc.reshape(b, lc, uc.shape[-1])

        o_na, o_na_ctx = _na_attention(_na_pair_bias(na_rpb[l]), u3, uc3,
                                       ctx_k_off=offs['na_k'], ctx_v_off=offs['na_v'], ctx_q_off=na_q)
        o_ret, o_ret_ctx = _retention(ret_decay_logit[l], cosf, sinf, u3, uc3,
                                      ctx_k_off=offs['ret_k'], ctx_v_off=offs['ret_v'], ctx_q_off=ret_q)

        wna = w_proj_na[l].astype(BF16)
        wret = w_proj_ret[l].astype(BF16)
        wout = w_out[l].astype(BF16)
        x_lat = _merge(o_na.reshape(b * s, -1), o_ret.reshape(b * s, -1), u, x_lat, gate,
                       wna, wret, wout, final_g.reshape(1, d) if last else None,
                       rows_per_mod=s, mod_row=None, tm=MERGE_TM)
        if not last:
            x_ctx = _merge(o_na_ctx.reshape(b * lc, -1), o_ret_ctx.reshape(b * lc, -1), uc, x_ctx, gate,
                           wna, wret, wout, None, rows_per_mod=lc, mod_row=CTX_MOD_ROW, tm=MERGE_TM)
    return x_lat.reshape(b, s, d)
```

```python
import functools

import jax
import jax.numpy as jnp
from jax import lax
from jax.experimental import pallas as pl
from jax.experimental.pallas import tpu as pltpu

F32 = jnp.float32
BF16 = jnp.bfloat16

D_MODEL = 2048
GRID_W = 64
N_HEADS = 8
NA_DIM = 128
NA_WIN_ROWS = 8
NA_WIN_COLS = 16
RET_KEY_DIM = 128
RET_VAL_DIM = 256
RET_CHUNK = 128
ROPE_BASE = 10000.0
NORM_EPS = 1e-6
MASK_VALUE = -1e30
LOG2E = 1.4426950408889634

OFF_NA_Q, OFF_NA_K, OFF_NA_V, OFF_NA_Z = 0, 1024, 2048, 3072
OFF_RET_Q, OFF_RET_K, OFF_RET_V, OFF_RET_Z = 4096, 5120, 6144, 8192
OFF_G_NA, OFF_G_RET = 10240, 12288
IN_COLS = 14336
CTX_KV_COLS = 5120
COFF_NA_K, COFF_NA_V, COFF_RET_K, COFF_RET_V = 0, 1024, 2048, 3072

MOD_ROWS = 16
CTX_MOD_ROW = 8

NA_QBLK = 4 * GRID_W
NA_KBLK = 12 * GRID_W

VMEM_LIMIT = 56 * 1024 * 1024
PRENORM_TM = 512
INPROJ_TM = 2048
INPROJ_TN = 1024
INPROJ_ROWS = 512
MERGE_TM = 256
NA_PROB_ROWS = 32
NA_SLOTS = 2


def _dot(a, b):
    return jnp.dot(a, b, preferred_element_type=F32)


def _dot_nt(a, b):
    return lax.dot_general(a, b, (((1,), (1,)), ((), ())), preferred_element_type=F32)


def _dot_tn(a, b):
    return lax.dot_general(a, b, (((0,), (0,)), ((), ())), preferred_element_type=F32)


def _sigmoid(x):
    return 1.0 / (1.0 + jnp.exp(-x))


def _mod_kernel(c_ref, w_ref, b_ref, o_ref):
    c = c_ref[...]
    cs = (c * _sigmoid(c)).astype(BF16)
    o_ref[0] = _dot(cs, w_ref[0].astype(BF16)) + b_ref[0]


def _modulation(c_all, ada_w, ada_b, tn=512):
    depth, d, n = ada_w.shape
    return pl.pallas_call(
        _mod_kernel,
        grid=(depth, n // tn),
        in_specs=[
            pl.BlockSpec((MOD_ROWS, d), lambda l, j: (0, 0)),
            pl.BlockSpec((1, d, tn), lambda l, j: (l, 0, j)),
            pl.BlockSpec((1, 1, tn), lambda l, j: (l, 0, j)),
        ],
        out_specs=pl.BlockSpec((1, MOD_ROWS, tn), lambda l, j: (l, 0, j)),
        out_shape=jax.ShapeDtypeStruct((depth, MOD_ROWS, n), F32),
        compiler_params=pltpu.CompilerParams(
            dimension_semantics=("arbitrary", "arbitrary"), vmem_limit_bytes=VMEM_LIMIT),
        name="adaln_mod",
    )(c_all, ada_w, ada_b.reshape(depth, 1, n))


def _modulated_norm(x, g, scale, shift):
    ms = jnp.mean(x * x, axis=-1, keepdims=True)
    return (x * lax.rsqrt(ms + NORM_EPS) * g) * (1.0 + scale) + shift


def _prenorm_kernel(x_ref, shift_ref, scale_ref, g_ref, o_ref):
    o_ref[...] = _modulated_norm(x_ref[...], g_ref[...], scale_ref[0], shift_ref[0]).astype(o_ref.dtype)


def _mod_index(tm, rows_per_mod, mod_row):
    if mod_row is None:
        return lambda i: ((i * tm) // rows_per_mod, 0, 0)
    return lambda i: (mod_row, 0, 0)


def _prenorm(x2, shift, scale, g, *, rows_per_mod, mod_row, tm):
    m, d = x2.shape
    mod_idx = _mod_index(tm, rows_per_mod, mod_row)
    return pl.pallas_call(
        _prenorm_kernel,
        grid=(m // tm,),
        in_specs=[
            pl.BlockSpec((tm, d), lambda i: (i, 0)),
            pl.BlockSpec((1, 1, d), mod_idx),
            pl.BlockSpec((1, 1, d), mod_idx),
            pl.BlockSpec((1, d), lambda i: (0, 0)),
        ],
        out_specs=pl.BlockSpec((tm, d), lambda i: (i, 0)),
        out_shape=jax.ShapeDtypeStruct((m, d), BF16),
        compiler_params=pltpu.CompilerParams(
            dimension_semantics=("arbitrary",), vmem_limit_bytes=VMEM_LIMIT),
        name="prenorm",
    )(x2, shift, scale, g)


def _inproj_kernel(*refs, rope_tiles):
    if rope_tiles is None:
        h_ref, w_ref, cs_ref, o_ref, wb_ref = refs
    else:
        h_ref, w_ref, cs_ref, cos_ref, sin_ref, o_ref, wb_ref = refs

    @pl.when(pl.program_id(1) == 0)
    def _():
        wb_ref[...] = w_ref[...].astype(BF16)

    row_chunks = [slice(r0, r0 + INPROJ_ROWS) for r0 in range(0, h_ref.shape[0], INPROJ_ROWS)]

    def project(rows):
        return _dot(h_ref[rows, :], wb_ref[...]) * cs_ref[...]

    def plain():
        for rows in row_chunks:
            o_ref[rows, :] = project(rows).astype(o_ref.dtype)

    if rope_tiles is None:
        plain()
        return

    j = pl.program_id(0)
    is_rope = jnp.logical_and(j >= rope_tiles[0], j < rope_tiles[1])
    pl.when(jnp.logical_not(is_rope))(plain)

    @pl.when(is_rope)
    def _():
        for rows in row_chunks:
            acc = project(rows)
            cos, sin = cos_ref[rows, :], sin_ref[rows, :]
            for c0 in range(0, acc.shape[1], RET_KEY_DIM):
                a = acc[:, c0:c0 + RET_KEY_DIM]
                o_ref[rows, c0:c0 + RET_KEY_DIM] = (
                    a * cos + pltpu.roll(a, RET_KEY_DIM // 2, 1) * sin).astype(o_ref.dtype)


def _inproj(h2, w_in, layer, col_scale, rope, *, col_map, n_out, tm, tn):
    m, d = h2.shape
    in_specs = [
        pl.BlockSpec((tm, d), lambda j, i: (i, 0)),
        pl.BlockSpec((None, d, tn), lambda j, i: (layer, 0, col_map(j))),
        pl.BlockSpec((1, tn), lambda j, i: (0, col_map(j))),
    ]
    args = [h2, w_in, col_scale]
    rope_tiles = None
    if rope is not None:
        cosf, sinf = rope
        seq_tiles = cosf.shape[0] // tm
        in_specs += [pl.BlockSpec((tm, RET_KEY_DIM), lambda j, i: (i % seq_tiles, 0))] * 2
        args += [cosf, sinf]
        rope_tiles = (OFF_RET_Q // tn, OFF_RET_V // tn)
    return pl.pallas_call(
        functools.partial(_inproj_kernel, rope_tiles=rope_tiles),
        grid=(n_out // tn, m // tm),
        in_specs=in_specs,
        out_specs=pl.BlockSpec((tm, tn), lambda j, i: (i, j)),
        out_shape=jax.ShapeDtypeStruct((m, n_out), BF16),
        scratch_shapes=[pltpu.VMEM((d, tn), BF16)],
        compiler_params=pltpu.CompilerParams(
            dimension_semantics=("arbitrary", "arbitrary"), vmem_limit_bytes=VMEM_LIMIT),
        name="inproj",
    )(*args)


def _na_window_row0(r, rows):
    return min(max(r - NA_WIN_ROWS // 2, 0), rows - NA_WIN_ROWS)


def _na_key_row0(i, rows):
    return min(max(4 * i - 4, 0), rows - 12)


def _na_kernel(*refs, rows, with_ctx_q):
    if with_ctx_q:
        (pb_ref, q_ref, k_ref, v_ref, kc_ref, vc_ref, qc_ref, o_ref, oc_ref,
         bias_ref, vx_ref, vcx_ref, s_ref, p_ref) = refs
    else:
        (pb_ref, q_ref, k_ref, v_ref, kc_ref, vc_ref, o_ref,
         bias_ref, vx_ref, vcx_ref, s_ref, p_ref) = refs
    n_blk = rows // 4

    @pl.when(pl.program_id(1) == 0)
    def _():
        lane = lax.broadcasted_iota(jnp.int32, (GRID_W, 2 * GRID_W), 1)
        neg = jnp.full((GRID_W, 2 * GRID_W), MASK_VALUE, F32)
        for ti, i in enumerate((0, 1, n_blk - 1)):
            krow0 = _na_key_row0(i, rows)
            for rl in range(4):
                r = 4 * i + rl
                r0 = _na_window_row0(r, rows)
                for j in range(6):
                    ka = krow0 + 2 * j
                    in_a = r0 <= ka < r0 + NA_WIN_ROWS
                    in_b = r0 <= ka + 1 < r0 + NA_WIN_ROWS
                    if in_a or in_b:
                        t = pb_ref[0, ka - r + NA_WIN_ROWS]
                        if not in_b:
                            t = jnp.where(lane < GRID_W, t, neg)
                        if not in_a:
                            t = jnp.where(lane >= GRID_W, t, neg)
                    else:
                        t = neg
                    bias_ref[ti, rl * GRID_W:(rl + 1) * GRID_W, j * 128:(j + 1) * 128] = t

    vx_ref[:, 0:NA_DIM] = v_ref[0]
    vx_ref[:, NA_DIM:] = jnp.ones((vx_ref.shape[0], NA_DIM), BF16)
    vcx_ref[:, 0:NA_DIM] = vc_ref[0]
    vcx_ref[:, NA_DIM:] = jnp.ones((vcx_ref.shape[0], NA_DIM), BF16)
    kc = kc_ref[0]
    n_slots = s_ref.shape[0]
    half = NA_QBLK // 2

    def key_rows(i):
        k0 = _na_key_row0(i, rows) * GRID_W
        return slice(k0, k0 + NA_KBLK)

    def scores(i):
        q = q_ref[0, i * NA_QBLK:(i + 1) * NA_QBLK, :]
        ti = 0 if i == 0 else (2 if i == n_blk - 1 else 1)
        s_ref[i % n_slots, :, 0:NA_KBLK] = _dot_nt(q, k_ref[0, key_rows(i), :]) + bias_ref[ti]
        s_ref[i % n_slots, :, NA_KBLK:] = _dot_nt(q, kc)

    def probs(i):
        for r0 in range(0, NA_QBLK, NA_PROB_ROWS):
            s = s_ref[i % n_slots, r0:r0 + NA_PROB_ROWS, :]
            p_ref[i % n_slots, r0:r0 + NA_PROB_ROWS, :] = jnp.exp2(
                s - jnp.max(s, axis=-1, keepdims=True)).astype(BF16)

    def values(i):
        for r0 in (0, half):
            ox = (_dot(p_ref[i % n_slots, r0:r0 + half, 0:NA_KBLK], vx_ref[key_rows(i), :])
                  + _dot(p_ref[i % n_slots, r0:r0 + half, NA_KBLK:], vcx_ref[...]))
            o_ref[0, i * NA_QBLK + r0:i * NA_QBLK + r0 + half, :] = (
                ox[:, 0:NA_DIM] / ox[:, NA_DIM:]).astype(o_ref.dtype)

    for t in range(n_blk + 2):
        if 0 <= t - 2:
            values(t - 2)
        if 0 <= t - 1 < n_blk:
            probs(t - 1)
        if t < n_blk:
            scores(t)

    if with_ctx_q:
        sc = _dot_nt(qc_ref[0], kc)
        pc = jnp.exp2(sc - jnp.max(sc, axis=-1, keepdims=True)).astype(BF16)
        ox = _dot(pc, vcx_ref[...])
        oc_ref[0] = (ox[:, 0:NA_DIM] / ox[:, NA_DIM:]).astype(oc_ref.dtype)


def _na_attention(pb, u3, uc3, *, ctx_k_off, ctx_v_off, ctx_q_off):
    b, s, _ = u3.shape
    lc = uc3.shape[1]
    rows = s // GRID_W
    with_ctx_q = ctx_q_off is not None
    hb = lambda off: off // NA_DIM
    in_specs = [
        pl.BlockSpec((1,) + pb.shape[1:], lambda h, bb: (h, 0, 0, 0)),
        pl.BlockSpec((1, s, NA_DIM), lambda h, bb: (bb, 0, hb(OFF_NA_Q) + h)),
        pl.BlockSpec((1, s, NA_DIM), lambda h, bb: (bb, 0, hb(OFF_NA_K) + h)),
        pl.BlockSpec((1, s, NA_DIM), lambda h, bb: (bb, 0, hb(OFF_NA_V) + h)),
        pl.BlockSpec((1, lc, NA_DIM), lambda h, bb: (bb, 0, hb(ctx_k_off) + h)),
        pl.BlockSpec((1, lc, NA_DIM), lambda h, bb: (bb, 0, hb(ctx_v_off) + h)),
    ]
    args = [pb, u3, u3, u3, uc3, uc3]
    out_specs = [pl.BlockSpec((1, s, NA_DIM), lambda h, bb: (bb, 0, h))]
    out_shape = [jax.ShapeDtypeStruct((b, s, N_HEADS * NA_DIM), BF16)]
    if with_ctx_q:
        in_specs.append(pl.BlockSpec((1, lc, NA_DIM), lambda h, bb: (bb, 0, hb(ctx_q_off) + h)))
        args.append(uc3)
        out_specs.append(pl.BlockSpec((1, lc, NA_DIM), lambda h, bb: (bb, 0, h)))
        out_shape.append(jax.ShapeDtypeStruct((b, lc, N_HEADS * NA_DIM), BF16))
    outs = pl.pallas_call(
        functools.partial(_na_kernel, rows=rows, with_ctx_q=with_ctx_q),
        grid=(N_HEADS, b),
        in_specs=in_specs,
        out_specs=out_specs,
        out_shape=out_shape,
        scratch_shapes=[
            pltpu.VMEM((3, NA_QBLK, NA_KBLK), F32),
            pltpu.VMEM((s, 2 * NA_DIM), BF16),
            pltpu.VMEM((lc, 2 * NA_DIM), BF16),
            pltpu.VMEM((NA_SLOTS, NA_QBLK, NA_KBLK + lc), F32),
            pltpu.VMEM((NA_SLOTS, NA_QBLK, NA_KBLK + lc), BF16),
        ],
        compiler_params=pltpu.CompilerParams(
            dimension_semantics=("arbitrary", "arbitrary"), vmem_limit_bytes=VMEM_LIMIT),
        name="na_attention",
    )(*args)
    return outs if with_ctx_q else (outs[0], None)


def _na_pair_bias(rpb):
    cq = jnp.arange(GRID_W)[:, None]
    ck = jnp.arange(GRID_W)[None, :]
    c0 = jnp.clip(cq - NA_WIN_COLS // 2, 0, GRID_W - NA_WIN_COLS)
    col_in = (ck >= c0) & (ck < c0 + NA_WIN_COLS)
    dc = jnp.clip(ck - cq + (NA_WIN_COLS - 1), 0, 2 * NA_WIN_COLS - 2)
    onehot = ((dc[None] == jnp.arange(2 * NA_WIN_COLS - 1)[:, None, None]) & col_in[None]).astype(F32)
    cb = jnp.einsum('hrd,dqk->hrqk', rpb.astype(F32), onehot, precision=lax.Precision.HIGHEST)
    cb = jnp.where(col_in[None, None], cb * LOG2E, MASK_VALUE)
    neg = jnp.full_like(cb[:, :1], MASK_VALUE)
    cbx = jnp.concatenate([neg, cb, neg], axis=1)
    return jnp.concatenate([cbx[:, :-1], cbx[:, 1:]], axis=-1)


def _ret_kernel(*refs, with_ctx_q):
    if with_ctx_q:
        (dl_ref, q_ref, k_ref, v_ref, kc_ref, vc_ref, qc_ref,
         o_ref, oc_ref, kv_ref, st_ref) = refs
    else:
        dl_ref, q_ref, k_ref, v_ref, kc_ref, vc_ref, o_ref, kv_ref, st_ref = refs
        qc_ref = oc_ref = None
    cs = RET_CHUNK
    dk = RET_KEY_DIM

    dl = dl_ref[...]
    lg = -(jnp.maximum(-dl, 0.0) + jnp.log1p(jnp.exp(-jnp.abs(dl))))
    head = lax.broadcasted_iota(jnp.int32, dl.shape, 1)
    lgh = jnp.sum(jnp.where(head == pl.program_id(1), lg, 0.0), axis=1, keepdims=True)
    lgf = lgh[0:1, :]
    lgb = lgh[1:2, :]

    pos = lax.broadcasted_iota(jnp.int32, (cs, dk), 0).astype(F32)
    q_dec_f = jnp.exp((pos + 1.0) * lgf)
    q_dec_b = jnp.exp((cs - pos) * lgb)
    k_dec_f = jnp.exp((cs - 1.0 - pos) * lgf)
    k_dec_b = jnp.exp(pos * lgb)
    dec_f = jnp.exp(cs * lgf)
    dec_b = jnp.exp(cs * lgb)
    ii = lax.broadcasted_iota(jnp.int32, (cs, cs), 0)
    jj = lax.broadcasted_iota(jnp.int32, (cs, cs), 1)
    diff = (ii - jj).astype(F32)
    intra = jnp.where(diff > 0, jnp.exp(jnp.maximum(diff, 0.0) * lgf),
                      jnp.where(diff < 0, jnp.exp(jnp.maximum(-diff, 0.0) * lgb), 2.0))

    def run(qr, kr, vr, outr, n_tok, s0f, s0b):
        nc = n_tok // cs
        chunk = lambda t: slice(t * cs, (t + 1) * cs)

        for t in range(nc):
            sl = chunk(t)
            k = kr[0, sl, :].astype(F32)
            kk = jnp.concatenate([k * k_dec_f, k * k_dec_b], axis=1).astype(BF16)
            kv_ref[t] = _dot_tn(kk, vr[0, sl, :])
            if outr is not None:
                st_ref[t, 0:cs, :] = vr[0, sl, :]

        sf = s0f
        for t in range(nc):
            st_ref[t, cs:cs + dk, :] = sf.astype(BF16)
            sf = sf * dec_f + kv_ref[t, 0:dk, :]
        sb = s0b
        for t in reversed(range(nc)):
            st_ref[t, cs + dk:cs + 2 * dk, :] = sb.astype(BF16)
            sb = sb * dec_b + kv_ref[t, dk:2 * dk, :]

        if outr is not None:
            for t in range(nc):
                sl = chunk(t)
                qb = qr[0, sl, :]
                q = qb.astype(F32)
                a = _dot_nt(qb, kr[0, sl, :])
                lhs = jnp.concatenate([a * intra, q * q_dec_f, q * q_dec_b], axis=1).astype(BF16)
                o = _dot(lhs, st_ref[t])
                o = o * lax.rsqrt(jnp.mean(o * o, axis=-1, keepdims=True) + NORM_EPS)
                outr[0, sl, :] = o.astype(outr.dtype)
        return sf, sb

    zero = jnp.zeros((dk, RET_VAL_DIM), F32)
    sf, sb = run(qc_ref, kc_ref, vc_ref, oc_ref, kc_ref.shape[1], zero, zero)
    run(q_ref, k_ref, v_ref, o_ref, k_ref.shape[1], sf, sb)


def _retention(decay_logit, u3, uc3, *, ctx_k_off, ctx_v_off, ctx_q_off):
    b, s, _ = u3.shape
    lc = uc3.shape[1]
    with_ctx_q = ctx_q_off is not None
    kb = lambda off: off // RET_KEY_DIM
    vb = lambda off: off // RET_VAL_DIM
    in_specs = [
        pl.BlockSpec(decay_logit.shape, lambda bb, h: (0, 0)),
        pl.BlockSpec((1, s, RET_KEY_DIM), lambda bb, h: (bb, 0, kb(OFF_RET_Q) + h)),
        pl.BlockSpec((1, s, RET_KEY_DIM), lambda bb, h: (bb, 0, kb(OFF_RET_K) + h)),
        pl.BlockSpec((1, s, RET_VAL_DIM), lambda bb, h: (bb, 0, vb(OFF_RET_V) + h)),
        pl.BlockSpec((1, lc, RET_KEY_DIM), lambda bb, h: (bb, 0, kb(ctx_k_off) + h)),
        pl.BlockSpec((1, lc, RET_VAL_DIM), lambda bb, h: (bb, 0, vb(ctx_v_off) + h)),
    ]
    args = [decay_logit, u3, u3, u3, uc3, uc3]
    out_specs = [pl.BlockSpec((1, s, RET_VAL_DIM), lambda bb, h: (bb, 0, h))]
    out_shape = [jax.ShapeDtypeStruct((b, s, N_HEADS * RET_VAL_DIM), BF16)]
    if with_ctx_q:
        in_specs.append(pl.BlockSpec((1, lc, RET_KEY_DIM), lambda bb, h: (bb, 0, kb(ctx_q_off) + h)))
        args.append(uc3)
        out_specs.append(pl.BlockSpec((1, lc, RET_VAL_DIM), lambda bb, h: (bb, 0, h)))
        out_shape.append(jax.ShapeDtypeStruct((b, lc, N_HEADS * RET_VAL_DIM), BF16))
    nc = s // RET_CHUNK
    outs = pl.pallas_call(
        functools.partial(_ret_kernel, with_ctx_q=with_ctx_q),
        grid=(b, N_HEADS),
        in_specs=in_specs,
        out_specs=out_specs,
        out_shape=out_shape,
        scratch_shapes=[
            pltpu.VMEM((nc, 2 * RET_KEY_DIM, RET_VAL_DIM), F32),
            pltpu.VMEM((nc, RET_CHUNK + 2 * RET_KEY_DIM, RET_VAL_DIM), BF16),
        ],
        compiler_params=pltpu.CompilerParams(
            dimension_semantics=("arbitrary", "arbitrary"), vmem_limit_bytes=VMEM_LIMIT),
        name="retention",
    )(*args)
    return outs if with_ctx_q else (outs[0], None)


def _merge_kernel(*refs, mode):
    (ona_ref, zna_ref, oret_ref, zret_ref, gna_ref, gret_ref, x_ref, gate_ref,
     wna_ref, wret_ref, wout_ref) = refs[:11]
    zna = zna_ref[...].astype(F32)
    a = (ona_ref[...].astype(F32) * (zna * _sigmoid(zna))).astype(BF16)
    y_na = _dot(a, wna_ref[...])
    zret = zret_ref[...].astype(F32)
    r = (oret_ref[...].astype(F32) * (zret * _sigmoid(zret))).astype(BF16)
    y_ret = _dot(r, wret_ref[...])
    merged = (_sigmoid(gna_ref[...].astype(F32)) * y_na
              + _sigmoid(gret_ref[...].astype(F32)) * y_ret).astype(BF16)
    xn = x_ref[...] + gate_ref[0] * _dot(merged, wout_ref[...])
    if mode == "final":
        fg_ref, o_ref = refs[11:]
        ms = jnp.mean(xn * xn, axis=-1, keepdims=True)
        o_ref[...] = xn * lax.rsqrt(ms + NORM_EPS) * fg_ref[...]
        return
    g_ref, scale_ref, shift_ref = refs[11:14]
    h_next = _modulated_norm(xn, g_ref[...], scale_ref[0], shift_ref[0])
    if mode == "stream+next":
        o_ref, h_ref = refs[14:]
        o_ref[...] = xn
    else:
        (h_ref,) = refs[14:]
    h_ref[...] = h_next.astype(h_ref.dtype)


def _merge(o_na, o_ret, u2, x2, gate, wna, wret, wout, *, mode, extra, rows_per_mod, mod_row, tm):
    m, d = x2.shape
    mod_idx = _mod_index(tm, rows_per_mod, mod_row)
    resident = lambda shape: pl.BlockSpec(shape, lambda i: (0, 0), pipeline_mode=pl.Buffered(1))
    row_tile = lambda width, col_blk: pl.BlockSpec((tm, width), lambda i: (i, col_blk))
    w_na_cols = N_HEADS * NA_DIM
    w_ret_cols = N_HEADS * RET_VAL_DIM
    in_specs = [
        row_tile(w_na_cols, 0),
        row_tile(w_na_cols, OFF_NA_Z // w_na_cols),
        row_tile(w_ret_cols, 0),
        row_tile(w_ret_cols, OFF_RET_Z // w_ret_cols),
        row_tile(d, OFF_G_NA // d),
        row_tile(d, OFF_G_RET // d),
        row_tile(d, 0),
        pl.BlockSpec((1, 1, d), mod_idx),
        resident(wna.shape),
        resident(wret.shape),
        resident(wout.shape),
    ]
    args = [o_na, u2, o_ret, u2, u2, u2, x2, gate, wna, wret, wout]
    stream = (row_tile(d, 0), jax.ShapeDtypeStruct((m, d), F32))
    nxt = (row_tile(d, 0), jax.ShapeDtypeStruct((m, d), BF16))
    if mode == "final":
        in_specs.append(pl.BlockSpec((1, d), lambda i: (0, 0)))
        args.append(extra)
        outs = [stream]
    else:
        g_next, scale_next, shift_next = extra
        in_specs += [pl.BlockSpec((1, d), lambda i: (0, 0)),
                     pl.BlockSpec((1, 1, d), mod_idx), pl.BlockSpec((1, 1, d), mod_idx)]
        args += [g_next, scale_next, shift_next]
        outs = [stream, nxt] if mode == "stream+next" else [nxt]
    return pl.pallas_call(
        functools.partial(_merge_kernel, mode=mode),
        grid=(m // tm,),
        in_specs=in_specs,
        out_specs=[o[0] for o in outs],
        out_shape=[o[1] for o in outs],
        compiler_params=pltpu.CompilerParams(
            dimension_semantics=("arbitrary",), vmem_limit_bytes=VMEM_LIMIT),
        name="merge",
    )(*args)


def _rope_tables(n_tokens):
    t = jnp.arange(n_tokens)
    row = (t // GRID_W).astype(F32)
    col = (t % GRID_W).astype(F32)
    n_freq = RET_KEY_DIM // 4
    inv_freq = ROPE_BASE ** (-jnp.arange(n_freq, dtype=F32) / n_freq)
    ang = jnp.concatenate([row[:, None] * inv_freq, col[:, None] * inv_freq], axis=-1)
    cos, sin = jnp.cos(ang), jnp.sin(ang)
    return jnp.concatenate([cos, cos], axis=-1), jnp.concatenate([-sin, sin], axis=-1)


def kernel(x, c, ctx, c_ctx, ada_w, ada_b, norm_g, w_in, na_rpb, ret_decay_logit,
           w_proj_na, w_proj_ret, w_out, final_g):
    b, s, d = x.shape
    lc = ctx.shape[1]
    depth = ada_w.shape[0]

    c_all = jnp.concatenate([c, c_ctx[None], jnp.zeros((MOD_ROWS - b - 1, d), F32)], axis=0)
    mod = _modulation(c_all, ada_w, ada_b)
    rope = _rope_tables(s)
    cols = jnp.arange(IN_COLS)
    col_scale = jnp.where((cols >= OFF_NA_Q) & (cols < OFF_NA_K), NA_DIM ** -0.5 * LOG2E,
                          jnp.where((cols >= OFF_RET_K) & (cols < OFF_RET_V), RET_KEY_DIM ** -0.5, 1.0)
                          ).astype(F32).reshape(1, IN_COLS)

    def modulation(l):
        shift = mod[l, :, 0:d].reshape(MOD_ROWS, 1, d)
        scale = mod[l, :, d:2 * d].reshape(MOD_ROWS, 1, d)
        gate = mod[l, :, 2 * d:3 * d].reshape(MOD_ROWS, 1, d)
        return norm_g[l].reshape(1, d), scale, shift, gate

    x_lat = x.reshape(b * s, d)
    x_ctx = ctx.reshape(b * lc, d)
    g, scale, shift, gate = modulation(0)
    h_lat = _prenorm(x_lat, shift, scale, g, rows_per_mod=s, mod_row=None, tm=PRENORM_TM)
    h_ctx = _prenorm(x_ctx, shift, scale, g, rows_per_mod=lc, mod_row=CTX_MOD_ROW, tm=PRENORM_TM)
    for l in range(depth):
        last = l == depth - 1
        u = _inproj(h_lat, w_in, l, col_scale, rope, col_map=lambda j: j, n_out=IN_COLS,
                    tm=INPROJ_TM, tn=INPROJ_TN)
        if last:
            uc = _inproj(h_ctx, w_in, l, col_scale, None,
                         col_map=lambda j: jnp.where(j < 2, j + 1, j + 3),
                         n_out=CTX_KV_COLS, tm=INPROJ_TM, tn=INPROJ_TN)
            offs = dict(na_k=COFF_NA_K, na_v=COFF_NA_V, ret_k=COFF_RET_K, ret_v=COFF_RET_V)
            na_q = ret_q = None
        else:
            uc = _inproj(h_ctx, w_in, l, col_scale, None, col_map=lambda j: j, n_out=IN_COLS,
                         tm=INPROJ_TM, tn=INPROJ_TN)
            offs = dict(na_k=OFF_NA_K, na_v=OFF_NA_V, ret_k=OFF_RET_K, ret_v=OFF_RET_V)
            na_q, ret_q = OFF_NA_Q, OFF_RET_Q
        u3 = u.reshape(b, s, IN_COLS)
        uc3 = uc.reshape(b, lc, uc.shape[-1])

        o_na, o_na_ctx = _na_attention(_na_pair_bias(na_rpb[l]), u3, uc3,
                                       ctx_k_off=offs['na_k'], ctx_v_off=offs['na_v'], ctx_q_off=na_q)
        o_ret, o_ret_ctx = _retention(ret_decay_logit[l], u3, uc3,
                                      ctx_k_off=offs['ret_k'], ctx_v_off=offs['ret_v'], ctx_q_off=ret_q)

        wna = w_proj_na[l].astype(BF16)
        wret = w_proj_ret[l].astype(BF16)
        wout = w_out[l].astype(BF16)
        lat_args = (o_na.reshape(b * s, -1), o_ret.reshape(b * s, -1), u, x_lat, gate, wna, wret, wout)
        if last:
            (x_lat,) = _merge(*lat_args, mode="final", extra=final_g.reshape(1, d),
                              rows_per_mod=s, mod_row=None, tm=MERGE_TM)
            break
        g, scale, shift, gate_next = modulation(l + 1)
        x_lat, h_lat = _merge(*lat_args, mode="stream+next", extra=(g, scale, shift),
                              rows_per_mod=s, mod_row=None, tm=MERGE_TM)
        ctx_args = (o_na_ctx.reshape(b * lc, -1), o_ret_ctx.reshape(b * lc, -1), uc, x_ctx,
                    gate, wna, wret, wout)
        if l + 1 == depth - 1:
            (h_ctx,) = _merge(*ctx_args, mode="next", extra=(g, scale, shift),
                              rows_per_mod=lc, mod_row=CTX_MOD_ROW, tm=MERGE_TM)
        else:
            x_ctx, h_ctx = _merge(*ctx_args, mode="stream+next", extra=(g, scale, shift),
                                  rows_per_mod=lc, mod_row=CTX_MOD_ROW, tm=MERGE_TM)
        gate = gate_next
    return x_lat.reshape(b, s, d)
```

```python
import functools

import jax
import jax.numpy as jnp
from jax import lax
from jax.experimental import pallas as pl
from jax.experimental.pallas import tpu as pltpu

F32 = jnp.float32
BF16 = jnp.bfloat16

D_MODEL = 2048
GRID_W = 64
N_HEADS = 8
NA_DIM = 128
NA_WIN_ROWS = 8
NA_WIN_COLS = 16
RET_KEY_DIM = 128
RET_VAL_DIM = 256
RET_CHUNK = 256
ROPE_BASE = 10000.0
NORM_EPS = 1e-6
MASK_VALUE = -1e30
LOG2E = 1.4426950408889634

OFF_NA_Q, OFF_NA_K, OFF_NA_V, OFF_NA_Z = 0, 1024, 2048, 3072
OFF_RET_Q, OFF_RET_K, OFF_RET_V, OFF_RET_Z = 4096, 5120, 6144, 8192
OFF_G_NA, OFF_G_RET = 10240, 12288
IN_COLS = 14336
CTX_KV_COLS = 5120
COFF_NA_K, COFF_NA_V, COFF_RET_K, COFF_RET_V = 0, 1024, 2048, 3072

MOD_ROWS = 16
CTX_MOD_ROW = 8

NA_QBLK = 4 * GRID_W
NA_KBLK = 12 * GRID_W

VMEM_LIMIT = 56 * 1024 * 1024
PRENORM_TM = 512
PRENORM_ROWS = 64
INPROJ_TM = 2048
INPROJ_TN = 1024
INPROJ_ROWS = 512
MERGE_TM = 256
MERGE_TN = 512
NA_PROB_ROWS = 32
NA_SLOTS = 2


def _dot(a, b):
    return jnp.dot(a, b, preferred_element_type=F32)


def _dot_nt(a, b):
    return lax.dot_general(a, b, (((1,), (1,)), ((), ())), preferred_element_type=F32)


def _dot_tn(a, b):
    return lax.dot_general(a, b, (((0,), (0,)), ((), ())), preferred_element_type=F32)


def _sigmoid(x):
    return 0.5 * jnp.tanh(0.5 * x) + 0.5


def _mod_kernel(c_ref, w_ref, b_ref, o_ref):
    c = c_ref[...]
    cs = (c * _sigmoid(c)).astype(BF16)
    o_ref[0] = _dot(cs, w_ref[0].astype(BF16)) + b_ref[0]


def _modulation(c_all, ada_w, ada_b, tn=512):
    depth, d, n = ada_w.shape
    return pl.pallas_call(
        _mod_kernel,
        grid=(depth, n // tn),
        in_specs=[
            pl.BlockSpec((MOD_ROWS, d), lambda l, j: (0, 0)),
            pl.BlockSpec((1, d, tn), lambda l, j: (l, 0, j)),
            pl.BlockSpec((1, 1, tn), lambda l, j: (l, 0, j)),
        ],
        out_specs=pl.BlockSpec((1, MOD_ROWS, tn), lambda l, j: (l, 0, j)),
        out_shape=jax.ShapeDtypeStruct((depth, MOD_ROWS, n), F32),
        compiler_params=pltpu.CompilerParams(
            dimension_semantics=("arbitrary", "arbitrary"), vmem_limit_bytes=VMEM_LIMIT),
        name="adaln_mod",
    )(c_all, ada_w, ada_b.reshape(depth, 1, n))


def _prenorm_kernel(x_ref, shift_ref, scale_ref, g_ref, o_ref, rinv_ref):
    n_chunks = x_ref.shape[0] // PRENORM_ROWS

    def rows(r):
        return pl.ds(pl.multiple_of(r * PRENORM_ROWS, PRENORM_ROWS), PRENORM_ROWS)

    def stats(r, carry):
        x = x_ref[rows(r), :]
        rinv_ref[rows(r), :] = lax.rsqrt(jnp.mean(x * x, axis=-1, keepdims=True) + NORM_EPS)
        return carry
    lax.fori_loop(0, n_chunks, stats, 0, unroll=2)

    gain = g_ref[...] * (1.0 + scale_ref[0])
    shift = shift_ref[0]

    def apply(r, carry):
        o_ref[rows(r), :] = (x_ref[rows(r), :] * rinv_ref[rows(r), :] * gain + shift).astype(o_ref.dtype)
        return carry
    lax.fori_loop(0, n_chunks, apply, 0, unroll=2)


def _mod_index(tm, rows_per_mod, mod_row):
    if mod_row is None:
        return lambda i: ((i * tm) // rows_per_mod, 0, 0)
    return lambda i: (mod_row, 0, 0)


def _prenorm(x2, shift, scale, g, *, rows_per_mod, mod_row, tm):
    m, d = x2.shape
    mod_idx = _mod_index(tm, rows_per_mod, mod_row)
    return pl.pallas_call(
        _prenorm_kernel,
        grid=(m // tm,),
        in_specs=[
            pl.BlockSpec((tm, d), lambda i: (i, 0)),
            pl.BlockSpec((1, 1, d), mod_idx),
            pl.BlockSpec((1, 1, d), mod_idx),
            pl.BlockSpec((1, d), lambda i: (0, 0)),
        ],
        out_specs=pl.BlockSpec((tm, d), lambda i: (i, 0)),
        out_shape=jax.ShapeDtypeStruct((m, d), BF16),
        scratch_shapes=[pltpu.VMEM((tm, 1), F32)],
        compiler_params=pltpu.CompilerParams(
            dimension_semantics=("arbitrary",), vmem_limit_bytes=VMEM_LIMIT),
        name="prenorm",
    )(x2, shift, scale, g)


def _inproj_kernel(*refs, rope_tiles):
    if rope_tiles is None:
        h_ref, w_ref, cs_ref, o_ref, wb_ref = refs
    else:
        h_ref, w_ref, cs_ref, cos_ref, sin_ref, o_ref, wb_ref = refs

    @pl.when(pl.program_id(1) == 0)
    def _():
        wb_ref[...] = w_ref[...].astype(BF16)

    row_chunks = [slice(r0, r0 + INPROJ_ROWS) for r0 in range(0, h_ref.shape[0], INPROJ_ROWS)]

    def project(rows):
        return _dot(h_ref[rows, :], wb_ref[...]) * cs_ref[...]

    def plain():
        for rows in row_chunks:
            o_ref[rows, :] = project(rows).astype(o_ref.dtype)

    if rope_tiles is None:
        plain()
        return

    j = pl.program_id(0)
    is_rope = jnp.logical_and(j >= rope_tiles[0], j < rope_tiles[1])
    pl.when(jnp.logical_not(is_rope))(plain)

    @pl.when(is_rope)
    def _():
        for rows in row_chunks:
            acc = project(rows)
            cos, sin = cos_ref[rows, :], sin_ref[rows, :]
            for c0 in range(0, acc.shape[1], RET_KEY_DIM):
                a = acc[:, c0:c0 + RET_KEY_DIM]
                o_ref[rows, c0:c0 + RET_KEY_DIM] = (
                    a * cos + pltpu.roll(a, RET_KEY_DIM // 2, 1) * sin).astype(o_ref.dtype)


def _inproj(h2, w_in, layer, col_scale, rope, *, col_map, n_out, tm, tn):
    m, d = h2.shape
    in_specs = [
        pl.BlockSpec((tm, d), lambda j, i: (i, 0)),
        pl.BlockSpec((None, d, tn), lambda j, i: (layer, 0, col_map(j))),
        pl.BlockSpec((1, tn), lambda j, i: (0, col_map(j))),
    ]
    args = [h2, w_in, col_scale]
    rope_tiles = None
    if rope is not None:
        cosf, sinf = rope
        seq_tiles = cosf.shape[0] // tm
        in_specs += [pl.BlockSpec((tm, RET_KEY_DIM), lambda j, i: (i % seq_tiles, 0))] * 2
        args += [cosf, sinf]
        rope_tiles = (OFF_RET_Q // tn, OFF_RET_V // tn)
    return pl.pallas_call(
        functools.partial(_inproj_kernel, rope_tiles=rope_tiles),
        grid=(n_out // tn, m // tm),
        in_specs=in_specs,
        out_specs=pl.BlockSpec((tm, tn), lambda j, i: (i, j)),
        out_shape=jax.ShapeDtypeStruct((m, n_out), BF16),
        scratch_shapes=[pltpu.VMEM((d, tn), BF16)],
        compiler_params=pltpu.CompilerParams(
            dimension_semantics=("arbitrary", "arbitrary"), vmem_limit_bytes=VMEM_LIMIT),
        name="inproj",
    )(*args)


def _na_window_row0(r, rows):
    return min(max(r - NA_WIN_ROWS // 2, 0), rows - NA_WIN_ROWS)


def _na_key_row0(i, rows):
    return min(max(4 * i - 4, 0), rows - 12)


def _na_kernel(*refs, rows, with_ctx_q):
    if with_ctx_q:
        (pb_ref, q_ref, k_ref, v_ref, kc_ref, vc_ref, qc_ref, o_ref, oc_ref,
         bias_ref, vx_ref, vcx_ref, s_ref, p_ref) = refs
    else:
        (pb_ref, q_ref, k_ref, v_ref, kc_ref, vc_ref, o_ref,
         bias_ref, vx_ref, vcx_ref, s_ref, p_ref) = refs
    n_blk = rows // 4

    @pl.when(pl.program_id(1) == 0)
    def _():
        lane = lax.broadcasted_iota(jnp.int32, (GRID_W, 2 * GRID_W), 1)
        neg = jnp.full((GRID_W, 2 * GRID_W), MASK_VALUE, F32)
        for ti, i in enumerate((0, 1, n_blk - 1)):
            krow0 = _na_key_row0(i, rows)
            for rl in range(4):
                r = 4 * i + rl
                r0 = _na_window_row0(r, rows)
                for j in range(6):
                    ka = krow0 + 2 * j
                    in_a = r0 <= ka < r0 + NA_WIN_ROWS
                    in_b = r0 <= ka + 1 < r0 + NA_WIN_ROWS
                    if in_a or in_b:
                        t = pb_ref[0, ka - r + NA_WIN_ROWS]
                        if not in_b:
                            t = jnp.where(lane < GRID_W, t, neg)
                        if not in_a:
                            t = jnp.where(lane >= GRID_W, t, neg)
                    else:
                        t = neg
                    bias_ref[ti, rl * GRID_W:(rl + 1) * GRID_W, j * 128:(j + 1) * 128] = t

    vx_ref[:, 0:NA_DIM] = v_ref[0]
    vx_ref[:, NA_DIM:] = jnp.ones((vx_ref.shape[0], NA_DIM), BF16)
    vcx_ref[:, 0:NA_DIM] = vc_ref[0]
    vcx_ref[:, NA_DIM:] = jnp.ones((vcx_ref.shape[0], NA_DIM), BF16)
    kc = kc_ref[0]
    n_slots = s_ref.shape[0]
    half = NA_QBLK // 2

    def key_rows(i):
        k0 = _na_key_row0(i, rows) * GRID_W
        return slice(k0, k0 + NA_KBLK)

    def scores(i):
        q = q_ref[0, i * NA_QBLK:(i + 1) * NA_QBLK, :]
        ti = 0 if i == 0 else (2 if i == n_blk - 1 else 1)
        s_ref[i % n_slots, :, 0:NA_KBLK] = _dot_nt(q, k_ref[0, key_rows(i), :]) + bias_ref[ti]
        s_ref[i % n_slots, :, NA_KBLK:] = _dot_nt(q, kc)

    def probs(i):
        for r0 in range(0, NA_QBLK, NA_PROB_ROWS):
            s = s_ref[i % n_slots, r0:r0 + NA_PROB_ROWS, :]
            p_ref[i % n_slots, r0:r0 + NA_PROB_ROWS, :] = jnp.exp2(
                s - jnp.max(s, axis=-1, keepdims=True)).astype(BF16)

    def values(i):
        for r0 in (0, half):
            ox = (_dot(p_ref[i % n_slots, r0:r0 + half, 0:NA_KBLK], vx_ref[key_rows(i), :])
                  + _dot(p_ref[i % n_slots, r0:r0 + half, NA_KBLK:], vcx_ref[...]))
            o_ref[0, i * NA_QBLK + r0:i * NA_QBLK + r0 + half, :] = (
                ox[:, 0:NA_DIM] / ox[:, NA_DIM:]).astype(o_ref.dtype)

    for t in range(n_blk + 2):
        if 0 <= t - 2:
            values(t - 2)
        if 0 <= t - 1 < n_blk:
            probs(t - 1)
        if t < n_blk:
            scores(t)

    if with_ctx_q:
        sc = _dot_nt(qc_ref[0], kc)
        pc = jnp.exp2(sc - jnp.max(sc, axis=-1, keepdims=True)).astype(BF16)
        ox = _dot(pc, vcx_ref[...])
        oc_ref[0] = (ox[:, 0:NA_DIM] / ox[:, NA_DIM:]).astype(oc_ref.dtype)


def _na_attention(pb, u3, uc3, *, ctx_k_off, ctx_v_off, ctx_q_off):
    b, s, _ = u3.shape
    lc = uc3.shape[1]
    rows = s // GRID_W
    with_ctx_q = ctx_q_off is not None
    hb = lambda off: off // NA_DIM
    in_specs = [
        pl.BlockSpec((1,) + pb.shape[1:], lambda h, bb: (h, 0, 0, 0)),
        pl.BlockSpec((1, s, NA_DIM), lambda h, bb: (bb, 0, hb(OFF_NA_Q) + h)),
        pl.BlockSpec((1, s, NA_DIM), lambda h, bb: (bb, 0, hb(OFF_NA_K) + h)),
        pl.BlockSpec((1, s, NA_DIM), lambda h, bb: (bb, 0, hb(OFF_NA_V) + h)),
        pl.BlockSpec((1, lc, NA_DIM), lambda h, bb: (bb, 0, hb(ctx_k_off) + h)),
        pl.BlockSpec((1, lc, NA_DIM), lambda h, bb: (bb, 0, hb(ctx_v_off) + h)),
    ]
    args = [pb, u3, u3, u3, uc3, uc3]
    out_specs = [pl.BlockSpec((1, s, NA_DIM), lambda h, bb: (bb, 0, h))]
    out_shape = [jax.ShapeDtypeStruct((b, s, N_HEADS * NA_DIM), BF16)]
    if with_ctx_q:
        in_specs.append(pl.BlockSpec((1, lc, NA_DIM), lambda h, bb: (bb, 0, hb(ctx_q_off) + h)))
        args.append(uc3)
        out_specs.append(pl.BlockSpec((1, lc, NA_DIM), lambda h, bb: (bb, 0, h)))
        out_shape.append(jax.ShapeDtypeStruct((b, lc, N_HEADS * NA_DIM), BF16))
    outs = pl.pallas_call(
        functools.partial(_na_kernel, rows=rows, with_ctx_q=with_ctx_q),
        grid=(N_HEADS, b),
        in_specs=in_specs,
        out_specs=out_specs,
        out_shape=out_shape,
        scratch_shapes=[
            pltpu.VMEM((3, NA_QBLK, NA_KBLK), F32),
            pltpu.VMEM((s, 2 * NA_DIM), BF16),
            pltpu.VMEM((lc, 2 * NA_DIM), BF16),
            pltpu.VMEM((NA_SLOTS, NA_QBLK, NA_KBLK + lc), F32),
            pltpu.VMEM((NA_SLOTS, NA_QBLK, NA_KBLK + lc), BF16),
        ],
        compiler_params=pltpu.CompilerParams(
            dimension_semantics=("arbitrary", "arbitrary"), vmem_limit_bytes=VMEM_LIMIT),
        name="na_attention",
    )(*args)
    return outs if with_ctx_q else (outs[0], None)


def _na_pair_bias(rpb):
    cq = jnp.arange(GRID_W)[:, None]
    ck = jnp.arange(GRID_W)[None, :]
    c0 = jnp.clip(cq - NA_WIN_COLS // 2, 0, GRID_W - NA_WIN_COLS)
    col_in = (ck >= c0) & (ck < c0 + NA_WIN_COLS)
    dc = jnp.clip(ck - cq + (NA_WIN_COLS - 1), 0, 2 * NA_WIN_COLS - 2)
    onehot = ((dc[None] == jnp.arange(2 * NA_WIN_COLS - 1)[:, None, None]) & col_in[None]).astype(F32)
    cb = jnp.einsum('hrd,dqk->hrqk', rpb.astype(F32), onehot, precision=lax.Precision.HIGHEST)
    cb = jnp.where(col_in[None, None], cb * LOG2E, MASK_VALUE)
    neg = jnp.full_like(cb[:, :1], MASK_VALUE)
    cbx = jnp.concatenate([neg, cb, neg], axis=1)
    return jnp.concatenate([cbx[:, :-1], cbx[:, 1:]], axis=-1)


def _ret_kernel(*refs, with_ctx_q):
    if with_ctx_q:
        (dl_ref, q_ref, k_ref, v_ref, kc_ref, vc_ref, qc_ref,
         o_ref, oc_ref, kv_ref, st_ref) = refs
    else:
        dl_ref, q_ref, k_ref, v_ref, kc_ref, vc_ref, o_ref, kv_ref, st_ref = refs
        qc_ref = oc_ref = None
    cs = RET_CHUNK
    dk = RET_KEY_DIM

    dl = dl_ref[...]
    lg = -(jnp.maximum(-dl, 0.0) + jnp.log1p(jnp.exp(-jnp.abs(dl))))
    head = lax.broadcasted_iota(jnp.int32, dl.shape, 1)
    lgh = jnp.sum(jnp.where(head == pl.program_id(1), lg, 0.0), axis=1, keepdims=True)
    lgf = lgh[0:1, :]
    lgb = lgh[1:2, :]

    pos = lax.broadcasted_iota(jnp.int32, (cs, dk), 0).astype(F32)
    q_dec_f = jnp.exp((pos + 1.0) * lgf)
    q_dec_b = jnp.exp((cs - pos) * lgb)
    k_dec_f = jnp.exp((cs - 1.0 - pos) * lgf)
    k_dec_b = jnp.exp(pos * lgb)
    dec_f = jnp.exp(cs * lgf)
    dec_b = jnp.exp(cs * lgb)
    ii = lax.broadcasted_iota(jnp.int32, (cs, cs), 0)
    jj = lax.broadcasted_iota(jnp.int32, (cs, cs), 1)
    diff = (ii - jj).astype(F32)
    intra = jnp.where(diff > 0, jnp.exp(jnp.maximum(diff, 0.0) * lgf),
                      jnp.where(diff < 0, jnp.exp(jnp.maximum(-diff, 0.0) * lgb), 2.0))

    def run(qr, kr, vr, outr, n_tok, s0f, s0b):
        nc = n_tok // cs
        chunk = lambda t: slice(t * cs, (t + 1) * cs)

        for t in range(nc):
            sl = chunk(t)
            k = kr[0, sl, :].astype(F32)
            kk = jnp.concatenate([k * k_dec_f, k * k_dec_b], axis=1).astype(BF16)
            kv_ref[t] = _dot_tn(kk, vr[0, sl, :])
            if outr is not None:
                st_ref[t, 0:cs, :] = vr[0, sl, :]

        sf = s0f
        for t in range(nc):
            st_ref[t, cs:cs + dk, :] = sf.astype(BF16)
            sf = sf * dec_f + kv_ref[t, 0:dk, :]
        sb = s0b
        for t in reversed(range(nc)):
            st_ref[t, cs + dk:cs + 2 * dk, :] = sb.astype(BF16)
            sb = sb * dec_b + kv_ref[t, dk:2 * dk, :]

        if outr is not None:
            for t in range(nc):
                sl = chunk(t)
                qb = qr[0, sl, :]
                q = qb.astype(F32)
                a = _dot_nt(qb, kr[0, sl, :])
                lhs = jnp.concatenate([a * intra, q * q_dec_f, q * q_dec_b], axis=1).astype(BF16)
                o = _dot(lhs, st_ref[t])
                o = o * lax.rsqrt(jnp.mean(o * o, axis=-1, keepdims=True) + NORM_EPS)
                outr[0, sl, :] = o.astype(outr.dtype)
        return sf, sb

    zero = jnp.zeros((dk, RET_VAL_DIM), F32)
    sf, sb = run(qc_ref, kc_ref, vc_ref, oc_ref, kc_ref.shape[1], zero, zero)
    run(q_ref, k_ref, v_ref, o_ref, k_ref.shape[1], sf, sb)


def _retention(decay_logit, u3, uc3, *, ctx_k_off, ctx_v_off, ctx_q_off):
    b, s, _ = u3.shape
    lc = uc3.shape[1]
    with_ctx_q = ctx_q_off is not None
    kb = lambda off: off // RET_KEY_DIM
    vb = lambda off: off // RET_VAL_DIM
    in_specs = [
        pl.BlockSpec(decay_logit.shape, lambda bb, h: (0, 0)),
        pl.BlockSpec((1, s, RET_KEY_DIM), lambda bb, h: (bb, 0, kb(OFF_RET_Q) + h)),
        pl.BlockSpec((1, s, RET_KEY_DIM), lambda bb, h: (bb, 0, kb(OFF_RET_K) + h)),
        pl.BlockSpec((1, s, RET_VAL_DIM), lambda bb, h: (bb, 0, vb(OFF_RET_V) + h)),
        pl.BlockSpec((1, lc, RET_KEY_DIM), lambda bb, h: (bb, 0, kb(ctx_k_off) + h)),
        pl.BlockSpec((1, lc, RET_VAL_DIM), lambda bb, h: (bb, 0, vb(ctx_v_off) + h)),
    ]
    args = [decay_logit, u3, u3, u3, uc3, uc3]
    out_specs = [pl.BlockSpec((1, s, RET_VAL_DIM), lambda bb, h: (bb, 0, h))]
    out_shape = [jax.ShapeDtypeStruct((b, s, N_HEADS * RET_VAL_DIM), BF16)]
    if with_ctx_q:
        in_specs.append(pl.BlockSpec((1, lc, RET_KEY_DIM), lambda bb, h: (bb, 0, kb(ctx_q_off) + h)))
        args.append(uc3)
        out_specs.append(pl.BlockSpec((1, lc, RET_VAL_DIM), lambda bb, h: (bb, 0, h)))
        out_shape.append(jax.ShapeDtypeStruct((b, lc, N_HEADS * RET_VAL_DIM), BF16))
    nc = s // RET_CHUNK
    outs = pl.pallas_call(
        functools.partial(_ret_kernel, with_ctx_q=with_ctx_q),
        grid=(b, N_HEADS),
        in_specs=in_specs,
        out_specs=out_specs,
        out_shape=out_shape,
        scratch_shapes=[
            pltpu.VMEM((nc, 2 * RET_KEY_DIM, RET_VAL_DIM), F32),
            pltpu.VMEM((nc, RET_CHUNK + 2 * RET_KEY_DIM, RET_VAL_DIM), BF16),
        ],
        compiler_params=pltpu.CompilerParams(
            dimension_semantics=("arbitrary", "arbitrary"), vmem_limit_bytes=VMEM_LIMIT),
        name="retention",
    )(*args)
    return outs if with_ctx_q else (outs[0], None)


def _merge_kernel(*refs, mode):
    (ona_ref, zna_ref, oret_ref, zret_ref, gna_ref, gret_ref, x_ref, gate_ref,
     wna_ref, wret_ref, wout_ref) = refs[:11]
    zna = zna_ref[...].astype(F32)
    a = (ona_ref[...].astype(F32) * (zna * _sigmoid(zna))).astype(BF16)
    y_na = _dot(a, wna_ref[...])
    zret = zret_ref[...].astype(F32)
    r = (oret_ref[...].astype(F32) * (zret * _sigmoid(zret))).astype(BF16)
    y_ret = _dot(r, wret_ref[...])
    merged = (_sigmoid(gna_ref[...].astype(F32)) * y_na
              + _sigmoid(gret_ref[...].astype(F32)) * y_ret).astype(BF16)
    if mode == "final":
        fg_ref, xn_ref = refs[11:]
    elif mode == "stream+next":
        g_ref, scale_ref, shift_ref, xn_ref, h_ref = refs[11:]
    else:
        g_ref, scale_ref, shift_ref, h_ref, xn_ref = refs[11:]

    d = x_ref.shape[1]
    ssq = jnp.zeros((x_ref.shape[0], 128), F32)
    for c0 in range(0, d, MERGE_TN):
        cols = slice(c0, c0 + MERGE_TN)
        xn = x_ref[:, cols] + gate_ref[0, :, cols] * _dot(merged, wout_ref[:, cols])
        xn_ref[:, cols] = xn
        for l0 in range(0, MERGE_TN, 128):
            ssq = ssq + xn[:, l0:l0 + 128] * xn[:, l0:l0 + 128]
    rinv = lax.rsqrt(jnp.sum(ssq, axis=-1, keepdims=True) * (1.0 / d) + NORM_EPS)
    if mode == "final":
        xn_ref[...] = xn_ref[...] * rinv * fg_ref[...]
    else:
        gain = g_ref[...] * (1.0 + scale_ref[0])
        h_ref[...] = (xn_ref[...] * rinv * gain + shift_ref[0]).astype(h_ref.dtype)


def _merge(o_na, o_ret, u2, x2, gate, wna, wret, wout, *, mode, extra, rows_per_mod, mod_row, tm):
    m, d = x2.shape
    mod_idx = _mod_index(tm, rows_per_mod, mod_row)
    resident = lambda shape: pl.BlockSpec(shape, lambda i: (0, 0), pipeline_mode=pl.Buffered(1))
    row_tile = lambda width, col_blk: pl.BlockSpec((tm, width), lambda i: (i, col_blk))
    w_na_cols = N_HEADS * NA_DIM
    w_ret_cols = N_HEADS * RET_VAL_DIM
    in_specs = [
        row_tile(w_na_cols, 0),
        row_tile(w_na_cols, OFF_NA_Z // w_na_cols),
        row_tile(w_ret_cols, 0),
        row_tile(w_ret_cols, OFF_RET_Z // w_ret_cols),
        row_tile(d, OFF_G_NA // d),
        row_tile(d, OFF_G_RET // d),
        row_tile(d, 0),
        pl.BlockSpec((1, 1, d), mod_idx),
        resident(wna.shape),
        resident(wret.shape),
        resident(wout.shape),
    ]
    args = [o_na, u2, o_ret, u2, u2, u2, x2, gate, wna, wret, wout]
    stream = (row_tile(d, 0), jax.ShapeDtypeStruct((m, d), F32))
    nxt = (row_tile(d, 0), jax.ShapeDtypeStruct((m, d), BF16))
    if mode == "final":
        in_specs.append(pl.BlockSpec((1, d), lambda i: (0, 0)))
        args.append(extra)
        outs = [stream]
    else:
        g_next, scale_next, shift_next = extra
        in_specs += [pl.BlockSpec((1, d), lambda i: (0, 0)),
                     pl.BlockSpec((1, 1, d), mod_idx), pl.BlockSpec((1, 1, d), mod_idx)]
        args += [g_next, scale_next, shift_next]
        outs = [stream, nxt] if mode == "stream+next" else [nxt]
    return pl.pallas_call(
        functools.partial(_merge_kernel, mode=mode),
        grid=(m // tm,),
        in_specs=in_specs,
        out_specs=[o[0] for o in outs],
        out_shape=[o[1] for o in outs],
        scratch_shapes=[pltpu.VMEM((tm, d), F32)] if mode == "next" else [],
        compiler_params=pltpu.CompilerParams(
            dimension_semantics=("arbitrary",), vmem_limit_bytes=VMEM_LIMIT),
        name="merge",
    )(*args)


def _rope_tables(n_tokens):
    t = jnp.arange(n_tokens)
    row = (t // GRID_W).astype(F32)
    col = (t % GRID_W).astype(F32)
    n_freq = RET_KEY_DIM // 4
    inv_freq = ROPE_BASE ** (-jnp.arange(n_freq, dtype=F32) / n_freq)
    ang = jnp.concatenate([row[:, None] * inv_freq, col[:, None] * inv_freq], axis=-1)
    cos, sin = jnp.cos(ang), jnp.sin(ang)
    return jnp.concatenate([cos, cos], axis=-1), jnp.concatenate([-sin, sin], axis=-1)


def kernel(x, c, ctx, c_ctx, ada_w, ada_b, norm_g, w_in, na_rpb, ret_decay_logit,
           w_proj_na, w_proj_ret, w_out, final_g):
    b, s, d = x.shape
    lc = ctx.shape[1]
    depth = ada_w.shape[0]

    c_all = jnp.concatenate([c, c_ctx[None], jnp.zeros((MOD_ROWS - b - 1, d), F32)], axis=0)
    mod = _modulation(c_all, ada_w, ada_b)
    rope = _rope_tables(s)
    cols = jnp.arange(IN_COLS)
    col_scale = jnp.where((cols >= OFF_NA_Q) & (cols < OFF_NA_K), NA_DIM ** -0.5 * LOG2E,
                          jnp.where((cols >= OFF_RET_K) & (cols < OFF_RET_V), RET_KEY_DIM ** -0.5, 1.0)
                          ).astype(F32).reshape(1, IN_COLS)

    def modulation(l):
        shift = mod[l, :, 0:d].reshape(MOD_ROWS, 1, d)
        scale = mod[l, :, d:2 * d].reshape(MOD_ROWS, 1, d)
        gate = mod[l, :, 2 * d:3 * d].reshape(MOD_ROWS, 1, d)
        return norm_g[l].reshape(1, d), scale, shift, gate

    x_lat = x.reshape(b * s, d)
    x_ctx = ctx.reshape(b * lc, d)
    g, scale, shift, gate = modulation(0)
    h_lat = _prenorm(x_lat, shift, scale, g, rows_per_mod=s, mod_row=None, tm=PRENORM_TM)
    h_ctx = _prenorm(x_ctx, shift, scale, g, rows_per_mod=lc, mod_row=CTX_MOD_ROW, tm=PRENORM_TM)
    for l in range(depth):
        last = l == depth - 1
        u = _inproj(h_lat, w_in, l, col_scale, rope, col_map=lambda j: j, n_out=IN_COLS,
                    tm=INPROJ_TM, tn=INPROJ_TN)
        if last:
            uc = _inproj(h_ctx, w_in, l, col_scale, None,
                         col_map=lambda j: jnp.where(j < 2, j + 1, j + 3),
                         n_out=CTX_KV_COLS, tm=INPROJ_TM, tn=INPROJ_TN)
            offs = dict(na_k=COFF_NA_K, na_v=COFF_NA_V, ret_k=COFF_RET_K, ret_v=COFF_RET_V)
            na_q = ret_q = None
        else:
            uc = _inproj(h_ctx, w_in, l, col_scale, None, col_map=lambda j: j, n_out=IN_COLS,
                         tm=INPROJ_TM, tn=INPROJ_TN)
            offs = dict(na_k=OFF_NA_K, na_v=OFF_NA_V, ret_k=OFF_RET_K, ret_v=OFF_RET_V)
            na_q, ret_q = OFF_NA_Q, OFF_RET_Q
        u3 = u.reshape(b, s, IN_COLS)
        uc3 = uc.reshape(b, lc, uc.shape[-1])

        o_na, o_na_ctx = _na_attention(_na_pair_bias(na_rpb[l]), u3, uc3,
                                       ctx_k_off=offs['na_k'], ctx_v_off=offs['na_v'], ctx_q_off=na_q)
        o_ret, o_ret_ctx = _retention(ret_decay_logit[l], u3, uc3,
                                      ctx_k_off=offs['ret_k'], ctx_v_off=offs['ret_v'], ctx_q_off=ret_q)

        wna = w_proj_na[l].astype(BF16)
        wret = w_proj_ret[l].astype(BF16)
        wout = w_out[l].astype(BF16)
        lat_args = (o_na.reshape(b * s, -1), o_ret.reshape(b * s, -1), u, x_lat, gate, wna, wret, wout)
        if last:
            (x_lat,) = _merge(*lat_args, mode="final", extra=final_g.reshape(1, d),
                              rows_per_mod=s, mod_row=None, tm=MERGE_TM)
            break
        g, scale, shift, gate_next = modulation(l + 1)
        x_lat, h_lat = _merge(*lat_args, mode="stream+next", extra=(g, scale, shift),
                              rows_per_mod=s, mod_row=None, tm=MERGE_TM)
        ctx_args = (o_na_ctx.reshape(b * lc, -1), o_ret_ctx.reshape(b * lc, -1), uc, x_ctx,
                    gate, wna, wret, wout)
        if l + 1 == depth - 1:
            (h_ctx,) = _merge(*ctx_args, mode="next", extra=(g, scale, shift),
                              rows_per_mod=lc, mod_row=CTX_MOD_ROW, tm=MERGE_TM)
        else:
            x_ctx, h_ctx = _merge(*ctx_args, mode="stream+next", extra=(g, scale, shift),
                                  rows_per_mod=lc, mod_row=CTX_MOD_ROW, tm=MERGE_TM)
        gate = gate_next
    return x_lat.reshape(b, s, d)
```

```python
import functools

import jax
import jax.numpy as jnp
from jax import lax
from jax.experimental import pallas as pl
from jax.experimental.pallas import tpu as pltpu

F32 = jnp.float32
BF16 = jnp.bfloat16

D_MODEL = 2048
GRID_W = 64
N_HEADS = 8
NA_DIM = 128
NA_WIN_ROWS = 8
NA_WIN_COLS = 16
RET_KEY_DIM = 128
RET_VAL_DIM = 256
RET_CHUNK = 256
ROPE_BASE = 10000.0
NORM_EPS = 1e-6
MASK_VALUE = -1e30
LOG2E = 1.4426950408889634

OFF_NA_Q, OFF_NA_K, OFF_NA_V, OFF_NA_Z = 0, 1024, 2048, 3072
OFF_RET_Q, OFF_RET_K, OFF_RET_V, OFF_RET_Z = 4096, 5120, 6144, 8192
OFF_G_NA, OFF_G_RET = 10240, 12288
IN_COLS = 14336
CTX_KV_COLS = 5120
COFF_NA_K, COFF_NA_V, COFF_RET_K, COFF_RET_V = 0, 1024, 2048, 3072

MOD_ROWS = 16
CTX_MOD_ROW = 8

NA_QBLK = 4 * GRID_W
NA_KBLK = 12 * GRID_W

VMEM_LIMIT = 56 * 1024 * 1024
PRENORM_TM = 1024
PRENORM_ROWS = 64
INPROJ_TM = 2048
INPROJ_TN = 1024
INPROJ_ROWS = 512
SIDE_CAST_ROWS = 64
MERGE_TM = 256
MERGE_TN = 512
NA_PROB_ROWS = 32
NA_SLOTS = 2


def _dot(a, b):
    return jnp.dot(a, b, preferred_element_type=F32)


def _dot_nt(a, b):
    return lax.dot_general(a, b, (((1,), (1,)), ((), ())), preferred_element_type=F32)


def _dot_tn(a, b):
    return lax.dot_general(a, b, (((0,), (0,)), ((), ())), preferred_element_type=F32)


def _sigmoid(x):
    return 0.5 * jnp.tanh(0.5 * x) + 0.5


def _silu(x):
    return x * _sigmoid(x)


def _mod_kernel(c_ref, w_ref, b_ref, o_ref):
    c = c_ref[...]
    cs = (c * _sigmoid(c)).astype(BF16)
    o_ref[0] = _dot(cs, w_ref[0].astype(BF16)) + b_ref[0]


def _modulation(c_all, ada_w, ada_b, tn=1024):
    depth, d, n = ada_w.shape
    return pl.pallas_call(
        _mod_kernel,
        grid=(depth, n // tn),
        in_specs=[
            pl.BlockSpec((MOD_ROWS, d), lambda l, j: (0, 0)),
            pl.BlockSpec((1, d, tn), lambda l, j: (l, 0, j)),
            pl.BlockSpec((1, 1, tn), lambda l, j: (l, 0, j)),
        ],
        out_specs=pl.BlockSpec((1, MOD_ROWS, tn), lambda l, j: (l, 0, j)),
        out_shape=jax.ShapeDtypeStruct((depth, MOD_ROWS, n), F32),
        compiler_params=pltpu.CompilerParams(
            dimension_semantics=("arbitrary", "arbitrary"), vmem_limit_bytes=VMEM_LIMIT),
        name="adaln_mod",
    )(c_all, ada_w, ada_b.reshape(depth, 1, n))


def _prenorm_kernel(x_ref, shift_ref, scale_ref, g_ref, o_ref, rinv_ref):
    n_chunks = x_ref.shape[0] // PRENORM_ROWS

    def rows(r):
        return pl.ds(pl.multiple_of(r * PRENORM_ROWS, PRENORM_ROWS), PRENORM_ROWS)

    def stats(r, carry):
        x = x_ref[rows(r), :]
        rinv_ref[rows(r), :] = lax.rsqrt(jnp.mean(x * x, axis=-1, keepdims=True) + NORM_EPS)
        return carry
    lax.fori_loop(0, n_chunks, stats, 0, unroll=2)

    gain = g_ref[...] * (1.0 + scale_ref[0])
    shift = shift_ref[0]

    def apply(r, carry):
        o_ref[rows(r), :] = (x_ref[rows(r), :] * rinv_ref[rows(r), :] * gain + shift).astype(o_ref.dtype)
        return carry
    lax.fori_loop(0, n_chunks, apply, 0, unroll=2)


def _mod_index(tm, rows_per_mod, mod_row):
    if mod_row is None:
        return lambda i: ((i * tm) // rows_per_mod, 0, 0)
    return lambda i: (mod_row, 0, 0)


def _prenorm(x2, shift, scale, g, *, rows_per_mod, mod_row, tm):
    m, d = x2.shape
    mod_idx = _mod_index(tm, rows_per_mod, mod_row)
    return pl.pallas_call(
        _prenorm_kernel,
        grid=(m // tm,),
        in_specs=[
            pl.BlockSpec((tm, d), lambda i: (i, 0)),
            pl.BlockSpec((1, 1, d), mod_idx),
            pl.BlockSpec((1, 1, d), mod_idx),
            pl.BlockSpec((1, d), lambda i: (0, 0)),
        ],
        out_specs=pl.BlockSpec((tm, d), lambda i: (i, 0)),
        out_shape=jax.ShapeDtypeStruct((m, d), BF16),
        scratch_shapes=[pltpu.VMEM((tm, 1), F32)],
        compiler_params=pltpu.CompilerParams(
            dimension_semantics=("arbitrary",), vmem_limit_bytes=VMEM_LIMIT),
        name="prenorm",
    )(x2, shift, scale, g)


def _inproj_kernel(*refs, rope_tiles, n_side):
    n_in = 3 + (0 if rope_tiles is None else 2)
    side_in = refs[n_in:n_in + n_side]
    side_out = refs[n_in + n_side + 1:n_in + 2 * n_side + 1]
    for src, dst in zip(side_in, side_out):
        dst[...] = src[...].astype(dst.dtype)
    refs = refs[:n_in] + (refs[n_in + n_side],) + refs[n_in + 2 * n_side + 1:]

    if rope_tiles is None:
        h_ref, w_ref, cs_ref, o_ref, wb_ref = refs
    else:
        h_ref, w_ref, cs_ref, cos_ref, sin_ref, o_ref, wb_ref = refs

    @pl.when(pl.program_id(1) == 0)
    def _():
        wb_ref[...] = w_ref[...].astype(BF16)

    row_chunks = [slice(r0, r0 + INPROJ_ROWS) for r0 in range(0, h_ref.shape[0], INPROJ_ROWS)]

    def project(rows):
        return _dot(h_ref[rows, :], wb_ref[...]) * cs_ref[...]

    def plain():
        for rows in row_chunks:
            o_ref[rows, :] = project(rows).astype(o_ref.dtype)

    if rope_tiles is None:
        plain()
        return

    j = pl.program_id(0)
    is_rope = jnp.logical_and(j >= rope_tiles[0], j < rope_tiles[1])
    pl.when(jnp.logical_not(is_rope))(plain)

    @pl.when(is_rope)
    def _():
        for rows in row_chunks:
            acc = project(rows)
            cos, sin = cos_ref[rows, :], sin_ref[rows, :]
            for c0 in range(0, acc.shape[1], RET_KEY_DIM):
                a = acc[:, c0:c0 + RET_KEY_DIM]
                o_ref[rows, c0:c0 + RET_KEY_DIM] = (
                    a * cos + pltpu.roll(a, RET_KEY_DIM // 2, 1) * sin).astype(o_ref.dtype)


def _inproj(h2, w_in, layer, col_scale, rope, *, col_map, n_out, tm, tn, side_weights=()):
    m, d = h2.shape
    n_i = m // tm
    in_specs = [
        pl.BlockSpec((tm, d), lambda j, i: (i, 0)),
        pl.BlockSpec((None, d, tn), lambda j, i: (layer, 0, col_map(j))),
        pl.BlockSpec((1, tn), lambda j, i: (0, col_map(j))),
    ]
    args = [h2, w_in, col_scale]
    rope_tiles = None
    if rope is not None:
        cosf, sinf = rope
        seq_tiles = cosf.shape[0] // tm
        in_specs += [pl.BlockSpec((tm, RET_KEY_DIM), lambda j, i: (i % seq_tiles, 0))] * 2
        args += [cosf, sinf]
        rope_tiles = (OFF_RET_Q // tn, OFF_RET_V // tn)
    out_specs = [pl.BlockSpec((tm, tn), lambda j, i: (i, j))]
    out_shape = [jax.ShapeDtypeStruct((m, n_out), BF16)]
    for w in side_weights:
        _, rows, cols = w.shape
        last = rows // SIDE_CAST_ROWS - 1
        assert last < (n_out // tn) * n_i, "not enough grid steps to cast this weight"
        chunk = lambda j, i, last=last: jnp.minimum(j * n_i + i, last)
        in_specs.append(pl.BlockSpec((None, SIDE_CAST_ROWS, cols),
                                     lambda j, i, chunk=chunk: (layer, chunk(j, i), 0)))
        args.append(w)
        out_specs.append(pl.BlockSpec((SIDE_CAST_ROWS, cols), lambda j, i, chunk=chunk: (chunk(j, i), 0)))
        out_shape.append(jax.ShapeDtypeStruct((rows, cols), BF16))
    return pl.pallas_call(
        functools.partial(_inproj_kernel, rope_tiles=rope_tiles, n_side=len(side_weights)),
        grid=(n_out // tn, n_i),
        in_specs=in_specs,
        out_specs=out_specs,
        out_shape=out_shape,
        scratch_shapes=[pltpu.VMEM((d, tn), BF16)],
        compiler_params=pltpu.CompilerParams(
            dimension_semantics=("arbitrary", "arbitrary"), vmem_limit_bytes=VMEM_LIMIT),
        name="inproj",
    )(*args)


def _na_window_row0(r, rows):
    return min(max(r - NA_WIN_ROWS // 2, 0), rows - NA_WIN_ROWS)


def _na_key_row0(i, rows):
    return min(max(4 * i - 4, 0), rows - 12)


def _na_kernel(*refs, rows, with_ctx_q):
    if with_ctx_q:
        (pb_ref, q_ref, k_ref, v_ref, kc_ref, vc_ref, qc_ref, o_ref, oc_ref,
         bias_ref, vx_ref, vcx_ref, s_ref, p_ref) = refs
    else:
        (pb_ref, q_ref, k_ref, v_ref, kc_ref, vc_ref, o_ref,
         bias_ref, vx_ref, vcx_ref, s_ref, p_ref) = refs
    n_blk = rows // 4

    @pl.when(pl.program_id(1) == 0)
    def _():
        lane = lax.broadcasted_iota(jnp.int32, (GRID_W, 2 * GRID_W), 1)
        neg = jnp.full((GRID_W, 2 * GRID_W), MASK_VALUE, F32)
        for ti, i in enumerate((0, 1, n_blk - 1)):
            krow0 = _na_key_row0(i, rows)
            for rl in range(4):
                r = 4 * i + rl
                r0 = _na_window_row0(r, rows)
                for j in range(6):
                    ka = krow0 + 2 * j
                    in_a = r0 <= ka < r0 + NA_WIN_ROWS
                    in_b = r0 <= ka + 1 < r0 + NA_WIN_ROWS
                    if in_a or in_b:
                        t = pb_ref[0, ka - r + NA_WIN_ROWS]
                        if not in_b:
                            t = jnp.where(lane < GRID_W, t, neg)
                        if not in_a:
                            t = jnp.where(lane >= GRID_W, t, neg)
                    else:
                        t = neg
                    bias_ref[ti, rl * GRID_W:(rl + 1) * GRID_W, j * 128:(j + 1) * 128] = t

    vx_ref[:, 0:NA_DIM] = v_ref[0]
    vx_ref[:, NA_DIM:] = jnp.ones((vx_ref.shape[0], NA_DIM), BF16)
    vcx_ref[:, 0:NA_DIM] = vc_ref[0]
    vcx_ref[:, NA_DIM:] = jnp.ones((vcx_ref.shape[0], NA_DIM), BF16)
    kc = kc_ref[0]
    n_slots = s_ref.shape[0]

    def key_rows(i):
        k0 = _na_key_row0(i, rows) * GRID_W
        return slice(k0, k0 + NA_KBLK)

    def scores(i):
        q = q_ref[0, i * NA_QBLK:(i + 1) * NA_QBLK, :]
        ti = 0 if i == 0 else (2 if i == n_blk - 1 else 1)
        s_ref[i % n_slots, :, 0:NA_KBLK] = _dot_nt(q, k_ref[0, key_rows(i), :]) + bias_ref[ti]
        s_ref[i % n_slots, :, NA_KBLK:] = _dot_nt(q, kc)

    def probs(i):
        for r0 in range(0, NA_QBLK, NA_PROB_ROWS):
            s = s_ref[i % n_slots, r0:r0 + NA_PROB_ROWS, :]
            p_ref[i % n_slots, r0:r0 + NA_PROB_ROWS, :] = jnp.exp2(
                s - jnp.max(s, axis=-1, keepdims=True)).astype(BF16)

    def values(i):
        k0 = key_rows(i).start
        ksplit = (NA_KBLK + kc.shape[0]) // 2
        p = p_ref.at[i % n_slots]
        ox = (_dot(p[:, 0:ksplit], vx_ref[k0:k0 + ksplit, :])
              + (_dot(p[:, ksplit:NA_KBLK], vx_ref[k0 + ksplit:k0 + NA_KBLK, :])
                 + _dot(p[:, NA_KBLK:], vcx_ref[...])))
        o_ref[0, i * NA_QBLK:(i + 1) * NA_QBLK, :] = (
            ox[:, 0:NA_DIM] / ox[:, NA_DIM:]).astype(o_ref.dtype)

    for t in range(n_blk + 2):
        if 0 <= t - 2:
            values(t - 2)
        if 0 <= t - 1 < n_blk:
            probs(t - 1)
        if t < n_blk:
            scores(t)

    if with_ctx_q:
        sc = _dot_nt(qc_ref[0], kc)
        pc = jnp.exp2(sc - jnp.max(sc, axis=-1, keepdims=True)).astype(BF16)
        ox = _dot(pc, vcx_ref[...])
        oc_ref[0] = (ox[:, 0:NA_DIM] / ox[:, NA_DIM:]).astype(oc_ref.dtype)


def _na_attention(pb, u3, uc3, *, ctx_k_off, ctx_v_off, ctx_q_off):
    b, s, _ = u3.shape
    lc = uc3.shape[1]
    rows = s // GRID_W
    with_ctx_q = ctx_q_off is not None
    hb = lambda off: off // NA_DIM
    in_specs = [
        pl.BlockSpec((1,) + pb.shape[1:], lambda h, bb: (h, 0, 0, 0)),
        pl.BlockSpec((1, s, NA_DIM), lambda h, bb: (bb, 0, hb(OFF_NA_Q) + h)),
        pl.BlockSpec((1, s, NA_DIM), lambda h, bb: (bb, 0, hb(OFF_NA_K) + h)),
        pl.BlockSpec((1, s, NA_DIM), lambda h, bb: (bb, 0, hb(OFF_NA_V) + h)),
        pl.BlockSpec((1, lc, NA_DIM), lambda h, bb: (bb, 0, hb(ctx_k_off) + h)),
        pl.BlockSpec((1, lc, NA_DIM), lambda h, bb: (bb, 0, hb(ctx_v_off) + h)),
    ]
    args = [pb, u3, u3, u3, uc3, uc3]
    out_specs = [pl.BlockSpec((1, s, NA_DIM), lambda h, bb: (bb, 0, h))]
    out_shape = [jax.ShapeDtypeStruct((b, s, N_HEADS * NA_DIM), BF16)]
    if with_ctx_q:
        in_specs.append(pl.BlockSpec((1, lc, NA_DIM), lambda h, bb: (bb, 0, hb(ctx_q_off) + h)))
        args.append(uc3)
        out_specs.append(pl.BlockSpec((1, lc, NA_DIM), lambda h, bb: (bb, 0, h)))
        out_shape.append(jax.ShapeDtypeStruct((b, lc, N_HEADS * NA_DIM), BF16))
    outs = pl.pallas_call(
        functools.partial(_na_kernel, rows=rows, with_ctx_q=with_ctx_q),
        grid=(N_HEADS, b),
        in_specs=in_specs,
        out_specs=out_specs,
        out_shape=out_shape,
        scratch_shapes=[
            pltpu.VMEM((3, NA_QBLK, NA_KBLK), F32),
            pltpu.VMEM((s, 2 * NA_DIM), BF16),
            pltpu.VMEM((lc, 2 * NA_DIM), BF16),
            pltpu.VMEM((NA_SLOTS, NA_QBLK, NA_KBLK + lc), F32),
            pltpu.VMEM((NA_SLOTS, NA_QBLK, NA_KBLK + lc), BF16),
        ],
        compiler_params=pltpu.CompilerParams(
            dimension_semantics=("arbitrary", "arbitrary"), vmem_limit_bytes=VMEM_LIMIT),
        name="na_attention",
    )(*args)
    return outs if with_ctx_q else (outs[0], None)


def _na_pair_bias(rpb):
    cq = jnp.arange(GRID_W)[:, None]
    ck = jnp.arange(GRID_W)[None, :]
    c0 = jnp.clip(cq - NA_WIN_COLS // 2, 0, GRID_W - NA_WIN_COLS)
    col_in = (ck >= c0) & (ck < c0 + NA_WIN_COLS)
    dc = jnp.clip(ck - cq + (NA_WIN_COLS - 1), 0, 2 * NA_WIN_COLS - 2)
    onehot = ((dc[None] == jnp.arange(2 * NA_WIN_COLS - 1)[:, None, None]) & col_in[None]).astype(F32)
    cb = jnp.einsum('hrd,dqk->hrqk', rpb.astype(F32), onehot, precision=lax.Precision.HIGHEST)
    cb = jnp.where(col_in[None, None], cb * LOG2E, MASK_VALUE)
    neg = jnp.full_like(cb[:, :1], MASK_VALUE)
    cbx = jnp.concatenate([neg, cb, neg], axis=1)
    return jnp.concatenate([cbx[:, :-1], cbx[:, 1:]], axis=-1)


def _ret_kernel(*refs, with_ctx_q):
    if with_ctx_q:
        (dl_ref, q_ref, k_ref, v_ref, kc_ref, vc_ref, qc_ref,
         o_ref, oc_ref, kv_ref, st_ref) = refs
    else:
        dl_ref, q_ref, k_ref, v_ref, kc_ref, vc_ref, o_ref, kv_ref, st_ref = refs
        qc_ref = oc_ref = None
    cs = RET_CHUNK
    dk = RET_KEY_DIM

    dl = dl_ref[...]
    lg = -(jnp.maximum(-dl, 0.0) + jnp.log1p(jnp.exp(-jnp.abs(dl))))
    head = lax.broadcasted_iota(jnp.int32, dl.shape, 1)
    lgh = jnp.sum(jnp.where(head == pl.program_id(1), lg, 0.0), axis=1, keepdims=True)
    lgf = lgh[0:1, :]
    lgb = lgh[1:2, :]

    pos = lax.broadcasted_iota(jnp.int32, (cs, dk), 0).astype(F32)
    q_dec_f = jnp.exp((pos + 1.0) * lgf)
    q_dec_b = jnp.exp((cs - pos) * lgb)
    k_dec_f = jnp.exp((cs - 1.0 - pos) * lgf)
    k_dec_b = jnp.exp(pos * lgb)
    dec_f = jnp.exp(cs * lgf)
    dec_b = jnp.exp(cs * lgb)
    ii = lax.broadcasted_iota(jnp.int32, (cs, cs), 0)
    jj = lax.broadcasted_iota(jnp.int32, (cs, cs), 1)
    diff = (ii - jj).astype(F32)
    intra = jnp.where(diff > 0, jnp.exp(jnp.maximum(diff, 0.0) * lgf),
                      jnp.where(diff < 0, jnp.exp(jnp.maximum(-diff, 0.0) * lgb), 2.0))

    def run(qr, kr, vr, outr, n_tok, s0f, s0b):
        nc = n_tok // cs
        chunk = lambda t: slice(t * cs, (t + 1) * cs)

        for t in range(nc):
            sl = chunk(t)
            k = kr[0, sl, :].astype(F32)
            kk = jnp.concatenate([k * k_dec_f, k * k_dec_b], axis=1).astype(BF16)
            kv_ref[t] = _dot_tn(kk, vr[0, sl, :])
            if outr is not None:
                st_ref[t, 0:cs, :] = vr[0, sl, :]

        sf = s0f
        for t in range(nc):
            st_ref[t, cs:cs + dk, :] = sf.astype(BF16)
            sf = sf * dec_f + kv_ref[t, 0:dk, :]
        sb = s0b
        for t in reversed(range(nc)):
            st_ref[t, cs + dk:cs + 2 * dk, :] = sb.astype(BF16)
            sb = sb * dec_b + kv_ref[t, dk:2 * dk, :]

        if outr is not None:
            for t in range(nc):
                sl = chunk(t)
                qb = qr[0, sl, :]
                q = qb.astype(F32)
                a = _dot_nt(qb, kr[0, sl, :])
                lhs = jnp.concatenate([a * intra, q * q_dec_f, q * q_dec_b], axis=1).astype(BF16)
                o = _dot(lhs, st_ref[t])
                o = o * lax.rsqrt(jnp.mean(o * o, axis=-1, keepdims=True) + NORM_EPS)
                outr[0, sl, :] = o.astype(outr.dtype)
        return sf, sb

    zero = jnp.zeros((dk, RET_VAL_DIM), F32)
    sf, sb = run(qc_ref, kc_ref, vc_ref, oc_ref, kc_ref.shape[1], zero, zero)
    run(q_ref, k_ref, v_ref, o_ref, k_ref.shape[1], sf, sb)


def _retention(decay_logit, u3, uc3, *, ctx_k_off, ctx_v_off, ctx_q_off):
    b, s, _ = u3.shape
    lc = uc3.shape[1]
    with_ctx_q = ctx_q_off is not None
    kb = lambda off: off // RET_KEY_DIM
    vb = lambda off: off // RET_VAL_DIM
    in_specs = [
        pl.BlockSpec(decay_logit.shape, lambda bb, h: (0, 0)),
        pl.BlockSpec((1, s, RET_KEY_DIM), lambda bb, h: (bb, 0, kb(OFF_RET_Q) + h)),
        pl.BlockSpec((1, s, RET_KEY_DIM), lambda bb, h: (bb, 0, kb(OFF_RET_K) + h)),
        pl.BlockSpec((1, s, RET_VAL_DIM), lambda bb, h: (bb, 0, vb(OFF_RET_V) + h)),
        pl.BlockSpec((1, lc, RET_KEY_DIM), lambda bb, h: (bb, 0, kb(ctx_k_off) + h)),
        pl.BlockSpec((1, lc, RET_VAL_DIM), lambda bb, h: (bb, 0, vb(ctx_v_off) + h)),
    ]
    args = [decay_logit, u3, u3, u3, uc3, uc3]
    out_specs = [pl.BlockSpec((1, s, RET_VAL_DIM), lambda bb, h: (bb, 0, h))]
    out_shape = [jax.ShapeDtypeStruct((b, s, N_HEADS * RET_VAL_DIM), BF16)]
    if with_ctx_q:
        in_specs.append(pl.BlockSpec((1, lc, RET_KEY_DIM), lambda bb, h: (bb, 0, kb(ctx_q_off) + h)))
        args.append(uc3)
        out_specs.append(pl.BlockSpec((1, lc, RET_VAL_DIM), lambda bb, h: (bb, 0, h)))
        out_shape.append(jax.ShapeDtypeStruct((b, lc, N_HEADS * RET_VAL_DIM), BF16))
    nc = s // RET_CHUNK
    outs = pl.pallas_call(
        functools.partial(_ret_kernel, with_ctx_q=with_ctx_q),
        grid=(b, N_HEADS),
        in_specs=in_specs,
        out_specs=out_specs,
        out_shape=out_shape,
        scratch_shapes=[
            pltpu.VMEM((nc, 2 * RET_KEY_DIM, RET_VAL_DIM), F32),
            pltpu.VMEM((nc, RET_CHUNK + 2 * RET_KEY_DIM, RET_VAL_DIM), BF16),
        ],
        compiler_params=pltpu.CompilerParams(
            dimension_semantics=("arbitrary", "arbitrary"), vmem_limit_bytes=VMEM_LIMIT),
        name="retention",
    )(*args)
    return outs if with_ctx_q else (outs[0], None)


def _merge_kernel(*refs, mode):
    (ona_ref, zna_ref, oret_ref, zret_ref, gna_ref, gret_ref, x_ref, gate_ref,
     wna_ref, wret_ref, wout_ref) = refs[:11]
    a = (ona_ref[...].astype(F32) * _silu(zna_ref[...].astype(F32))).astype(BF16)
    y_na = _dot(a, wna_ref[...])
    r = (oret_ref[...].astype(F32) * _silu(zret_ref[...].astype(F32))).astype(BF16)
    y_ret = _dot(r, wret_ref[...])
    merged = (_sigmoid(gna_ref[...].astype(F32)) * y_na
              + _sigmoid(gret_ref[...].astype(F32)) * y_ret).astype(BF16)
    if mode == "final":
        fg_ref, xn_ref = refs[11:]
    elif mode == "stream+next":
        g_ref, scale_ref, shift_ref, xn_ref, h_ref = refs[11:]
    else:
        g_ref, scale_ref, shift_ref, h_ref, xn_ref = refs[11:]

    d = x_ref.shape[1]
    ssq = jnp.zeros((x_ref.shape[0], 128), F32)
    for c0 in range(0, d, MERGE_TN):
        cols = slice(c0, c0 + MERGE_TN)
        xn = x_ref[:, cols] + gate_ref[0, :, cols] * _dot(merged, wout_ref[:, cols])
        xn_ref[:, cols] = xn
        for l0 in range(0, MERGE_TN, 128):
            ssq = ssq + xn[:, l0:l0 + 128] * xn[:, l0:l0 + 128]
    rinv = lax.rsqrt(jnp.sum(ssq, axis=-1, keepdims=True) * (1.0 / d) + NORM_EPS)
    if mode == "final":
        xn_ref[...] = xn_ref[...] * rinv * fg_ref[...]
    else:
        gain = g_ref[...] * (1.0 + scale_ref[0])
        h_ref[...] = (xn_ref[...] * rinv * gain + shift_ref[0]).astype(h_ref.dtype)


def _merge(o_na, o_ret, u2, x2, gate, wna, wret, wout, *, mode, extra, rows_per_mod, mod_row, tm):
    m, d = x2.shape
    mod_idx = _mod_index(tm, rows_per_mod, mod_row)
    resident = lambda shape: pl.BlockSpec(shape, lambda i: (0, 0), pipeline_mode=pl.Buffered(1))
    row_tile = lambda width, col_blk: pl.BlockSpec((tm, width), lambda i: (i, col_blk))
    w_na_cols = N_HEADS * NA_DIM
    w_ret_cols = N_HEADS * RET_VAL_DIM
    in_specs = [
        row_tile(w_na_cols, 0),
        row_tile(w_na_cols, OFF_NA_Z // w_na_cols),
        row_tile(w_ret_cols, 0),
        row_tile(w_ret_cols, OFF_RET_Z // w_ret_cols),
        row_tile(d, OFF_G_NA // d),
        row_tile(d, OFF_G_RET // d),
        row_tile(d, 0),
        pl.BlockSpec((1, 1, d), mod_idx),
        resident(wna.shape),
        resident(wret.shape),
        resident(wout.shape),
    ]
    args = [o_na, u2, o_ret, u2, u2, u2, x2, gate, wna, wret, wout]
    stream = (row_tile(d, 0), jax.ShapeDtypeStruct((m, d), F32))
    nxt = (row_tile(d, 0), jax.ShapeDtypeStruct((m, d), BF16))
    if mode == "final":
        in_specs.append(pl.BlockSpec((1, d), lambda i: (0, 0)))
        args.append(extra)
        outs = [stream]
    else:
        g_next, scale_next, shift_next = extra
        in_specs += [pl.BlockSpec((1, d), lambda i: (0, 0)),
                     pl.BlockSpec((1, 1, d), mod_idx), pl.BlockSpec((1, 1, d), mod_idx)]
        args += [g_next, scale_next, shift_next]
        outs = [stream, nxt] if mode == "stream+next" else [nxt]
    return pl.pallas_call(
        functools.partial(_merge_kernel, mode=mode),
        grid=(m // tm,),
        in_specs=in_specs,
        out_specs=[o[0] for o in outs],
        out_shape=[o[1] for o in outs],
        scratch_shapes=[pltpu.VMEM((tm, d), F32)] if mode == "next" else [],
        compiler_params=pltpu.CompilerParams(
            dimension_semantics=("arbitrary",), vmem_limit_bytes=VMEM_LIMIT),
        name="merge",
    )(*args)


def _rope_tables(n_tokens):
    t = jnp.arange(n_tokens)
    row = (t // GRID_W).astype(F32)
    col = (t % GRID_W).astype(F32)
    n_freq = RET_KEY_DIM // 4
    inv_freq = ROPE_BASE ** (-jnp.arange(n_freq, dtype=F32) / n_freq)
    ang = jnp.concatenate([row[:, None] * inv_freq, col[:, None] * inv_freq], axis=-1)
    cos, sin = jnp.cos(ang), jnp.sin(ang)
    return jnp.concatenate([cos, cos], axis=-1), jnp.concatenate([-sin, sin], axis=-1)


def kernel(x, c, ctx, c_ctx, ada_w, ada_b, norm_g, w_in, na_rpb, ret_decay_logit,
           w_proj_na, w_proj_ret, w_out, final_g):
    b, s, d = x.shape
    lc = ctx.shape[1]
    depth = ada_w.shape[0]

    c_all = jnp.concatenate([c, c_ctx[None], jnp.zeros((MOD_ROWS - b - 1, d), F32)], axis=0)
    mod = _modulation(c_all, ada_w, ada_b)
    rope = _rope_tables(s)
    cols = jnp.arange(IN_COLS)
    col_scale = jnp.where((cols >= OFF_NA_Q) & (cols < OFF_NA_K), NA_DIM ** -0.5 * LOG2E,
                          jnp.where((cols >= OFF_RET_K) & (cols < OFF_RET_V), RET_KEY_DIM ** -0.5, 1.0)
                          ).astype(F32).reshape(1, IN_COLS)

    def modulation(l):
        shift = mod[l, :, 0:d].reshape(MOD_ROWS, 1, d)
        scale = mod[l, :, d:2 * d].reshape(MOD_ROWS, 1, d)
        gate = mod[l, :, 2 * d:3 * d].reshape(MOD_ROWS, 1, d)
        return norm_g[l].reshape(1, d), scale, shift, gate

    x_lat = x.reshape(b * s, d)
    x_ctx = ctx.reshape(b * lc, d)
    g, scale, shift, gate = modulation(0)
    h_lat = _prenorm(x_lat, shift, scale, g, rows_per_mod=s, mod_row=None, tm=PRENORM_TM)
    h_ctx = _prenorm(x_ctx, shift, scale, g, rows_per_mod=lc, mod_row=CTX_MOD_ROW, tm=PRENORM_TM)
    for l in range(depth):
        last = l == depth - 1
        u, wna, wret, wout = _inproj(h_lat, w_in, l, col_scale, rope, col_map=lambda j: j,
                                     n_out=IN_COLS, tm=INPROJ_TM, tn=INPROJ_TN,
                                     side_weights=(w_proj_na, w_proj_ret, w_out))
        if last:
            (uc,) = _inproj(h_ctx, w_in, l, col_scale, None,
                            col_map=lambda j: jnp.where(j < 2, j + 1, j + 3),
                            n_out=CTX_KV_COLS, tm=INPROJ_TM, tn=INPROJ_TN)
            offs = dict(na_k=COFF_NA_K, na_v=COFF_NA_V, ret_k=COFF_RET_K, ret_v=COFF_RET_V)
            na_q = ret_q = None
        else:
            (uc,) = _inproj(h_ctx, w_in, l, col_scale, None, col_map=lambda j: j, n_out=IN_COLS,
                            tm=INPROJ_TM, tn=INPROJ_TN)
            offs = dict(na_k=OFF_NA_K, na_v=OFF_NA_V, ret_k=OFF_RET_K, ret_v=OFF_RET_V)
            na_q, ret_q = OFF_NA_Q, OFF_RET_Q
        u3 = u.reshape(b, s, IN_COLS)
        uc3 = uc.reshape(b, lc, uc.shape[-1])

        o_na, o_na_ctx = _na_attention(_na_pair_bias(na_rpb[l]), u3, uc3,
                                       ctx_k_off=offs['na_k'], ctx_v_off=offs['na_v'], ctx_q_off=na_q)
        o_ret, o_ret_ctx = _retention(ret_decay_logit[l], u3, uc3,
                                      ctx_k_off=offs['ret_k'], ctx_v_off=offs['ret_v'], ctx_q_off=ret_q)

        lat_args = (o_na.reshape(b * s, -1), o_ret.reshape(b * s, -1), u, x_lat, gate, wna, wret, wout)
        if last:
            (x_lat,) = _merge(*lat_args, mode="final", extra=final_g.reshape(1, d),
                              rows_per_mod=s, mod_row=None, tm=MERGE_TM)
            break
        g, scale, shift, gate_next = modulation(l + 1)
        x_lat, h_lat = _merge(*lat_args, mode="stream+next", extra=(g, scale, shift),
                              rows_per_mod=s, mod_row=None, tm=MERGE_TM)
        ctx_args = (o_na_ctx.reshape(b * lc, -1), o_ret_ctx.reshape(b * lc, -1), uc, x_ctx,
                    gate, wna, wret, wout)
        if l + 1 == depth - 1:
            (h_ctx,) = _merge(*ctx_args, mode="next", extra=(g, scale, shift),
                              rows_per_mod=lc, mod_row=CTX_MOD_ROW, tm=MERGE_TM)
        else:
            x_ctx, h_ctx = _merge(*ctx_args, mode="stream+next", extra=(g, scale, shift),
                                  rows_per_mod=lc, mod_row=CTX_MOD_ROW, tm=MERGE_TM)
        gate = gate_next
    return x_lat.reshape(b, s, d)
```

```python
import functools

import jax
import jax.numpy as jnp
from jax import lax
from jax.experimental import pallas as pl
from jax.experimental.pallas import tpu as pltpu

F32 = jnp.float32
BF16 = jnp.bfloat16

D_MODEL = 2048
GRID_W = 64
N_HEADS = 8
NA_DIM = 128
NA_WIN_ROWS = 8
NA_WIN_COLS = 16
RET_KEY_DIM = 128
RET_VAL_DIM = 256
RET_CHUNK = 256
ROPE_BASE = 10000.0
NORM_EPS = 1e-6
MASK_VALUE = -1e30
LOG2E = 1.4426950408889634

OFF_NA_Q, OFF_NA_K, OFF_NA_V, OFF_NA_Z = 0, 1024, 2048, 3072
OFF_RET_Q, OFF_RET_K, OFF_RET_V, OFF_RET_Z = 4096, 5120, 6144, 8192
OFF_G_NA, OFF_G_RET = 10240, 12288
IN_COLS = 14336
CTX_KV_COLS = 5120
COFF_NA_K, COFF_NA_V, COFF_RET_K, COFF_RET_V = 0, 1024, 2048, 3072

MOD_ROWS = 16
CTX_MOD_ROW = 8

NA_QBLK = 4 * GRID_W
NA_KBLK = 12 * GRID_W

VMEM_LIMIT = 56 * 1024 * 1024
PRENORM_TM = 1024
PRENORM_ROWS = 64
INPROJ_TM = 2048
INPROJ_TN = 1024
INPROJ_ROWS = 512
SIDE_CAST_ROWS = 64
MERGE_TM = 256
MERGE_TN = 512
NA_PROB_ROWS = 32
NA_SLOTS = 2


def _dot(a, b):
    return jnp.dot(a, b, preferred_element_type=F32)


def _dot_nt(a, b):
    return lax.dot_general(a, b, (((1,), (1,)), ((), ())), preferred_element_type=F32)


def _dot_tn(a, b):
    return lax.dot_general(a, b, (((0,), (0,)), ((), ())), preferred_element_type=F32)


def _sigmoid(x):
    return 0.5 * jnp.tanh(0.5 * x) + 0.5


def _silu(x):
    return x * _sigmoid(x)


def _mod_kernel(c_ref, w_ref, b_ref, o_ref):
    c = c_ref[...]
    cs = (c * _sigmoid(c)).astype(BF16)
    o_ref[0] = _dot(cs, w_ref[0].astype(BF16)) + b_ref[0]


def _modulation(c_all, ada_w, ada_b, tn=1024):
    depth, d, n = ada_w.shape
    return pl.pallas_call(
        _mod_kernel,
        grid=(depth, n // tn),
        in_specs=[
            pl.BlockSpec((MOD_ROWS, d), lambda l, j: (0, 0)),
            pl.BlockSpec((1, d, tn), lambda l, j: (l, 0, j)),
            pl.BlockSpec((1, 1, tn), lambda l, j: (l, 0, j)),
        ],
        out_specs=pl.BlockSpec((1, MOD_ROWS, tn), lambda l, j: (l, 0, j)),
        out_shape=jax.ShapeDtypeStruct((depth, MOD_ROWS, n), F32),
        compiler_params=pltpu.CompilerParams(
            dimension_semantics=("arbitrary", "arbitrary"), vmem_limit_bytes=VMEM_LIMIT),
        name="adaln_mod",
    )(c_all, ada_w, ada_b.reshape(depth, 1, n))


def _prenorm_kernel(x_ref, shift_ref, scale_ref, g_ref, o_ref, rinv_ref):
    n_chunks = x_ref.shape[0] // PRENORM_ROWS

    def rows(r):
        return pl.ds(pl.multiple_of(r * PRENORM_ROWS, PRENORM_ROWS), PRENORM_ROWS)

    def stats(r, carry):
        x = x_ref[rows(r), :]
        rinv_ref[rows(r), :] = lax.rsqrt(jnp.mean(x * x, axis=-1, keepdims=True) + NORM_EPS)
        return carry
    lax.fori_loop(0, n_chunks, stats, 0, unroll=2)

    gain = g_ref[...] * (1.0 + scale_ref[0])
    shift = shift_ref[0]

    def apply(r, carry):
        o_ref[rows(r), :] = (x_ref[rows(r), :] * rinv_ref[rows(r), :] * gain + shift).astype(o_ref.dtype)
        return carry
    lax.fori_loop(0, n_chunks, apply, 0, unroll=2)


def _mod_index(tm, rows_per_mod, mod_row):
    if mod_row is None:
        return lambda i: ((i * tm) // rows_per_mod, 0, 0)
    return lambda i: (mod_row, 0, 0)


def _prenorm(x2, shift, scale, g, *, rows_per_mod, mod_row, tm):
    m, d = x2.shape
    mod_idx = _mod_index(tm, rows_per_mod, mod_row)
    return pl.pallas_call(
        _prenorm_kernel,
        grid=(m // tm,),
        in_specs=[
            pl.BlockSpec((tm, d), lambda i: (i, 0)),
            pl.BlockSpec((1, 1, d), mod_idx),
            pl.BlockSpec((1, 1, d), mod_idx),
            pl.BlockSpec((1, d), lambda i: (0, 0)),
        ],
        out_specs=pl.BlockSpec((tm, d), lambda i: (i, 0)),
        out_shape=jax.ShapeDtypeStruct((m, d), BF16),
        scratch_shapes=[pltpu.VMEM((tm, 1), F32)],
        compiler_params=pltpu.CompilerParams(
            dimension_semantics=("arbitrary",), vmem_limit_bytes=VMEM_LIMIT),
        name="prenorm",
    )(x2, shift, scale, g)


def _inproj_kernel(*refs, epilogues, n_side):
    n_in = 3 + (2 if "rope" in epilogues else 0)
    side_in = refs[n_in:n_in + n_side]
    side_out = refs[n_in + n_side + 1:n_in + 2 * n_side + 1]
    for src, dst in zip(side_in, side_out):
        dst[...] = src[...].astype(dst.dtype)
    refs = refs[:n_in] + (refs[n_in + n_side],) + refs[n_in + 2 * n_side + 1:]

    if "rope" in epilogues:
        h_ref, w_ref, cs_ref, cos_ref, sin_ref, o_ref, wb_ref = refs
    else:
        h_ref, w_ref, cs_ref, o_ref, wb_ref = refs

    @pl.when(pl.program_id(1) == 0)
    def _():
        wb_ref[...] = w_ref[...].astype(BF16)

    row_chunks = [slice(r0, r0 + INPROJ_ROWS) for r0 in range(0, h_ref.shape[0], INPROJ_ROWS)]

    def project(rows):
        return _dot(h_ref[rows, :], wb_ref[...]) * cs_ref[...]

    def emit(kind):
        def run():
            for rows in row_chunks:
                acc = project(rows)
                if kind == "rope":
                    cos, sin = cos_ref[rows, :], sin_ref[rows, :]
                    for c0 in range(0, acc.shape[1], RET_KEY_DIM):
                        a = acc[:, c0:c0 + RET_KEY_DIM]
                        o_ref[rows, c0:c0 + RET_KEY_DIM] = (
                            a * cos + pltpu.roll(a, RET_KEY_DIM // 2, 1) * sin).astype(o_ref.dtype)
                elif kind == "silu":
                    o_ref[rows, :] = _silu(acc).astype(o_ref.dtype)
                elif kind == "sigmoid":
                    o_ref[rows, :] = _sigmoid(acc).astype(o_ref.dtype)
                else:
                    o_ref[rows, :] = acc.astype(o_ref.dtype)
        return run

    if not epilogues:
        emit("plain")()
        return

    j = pl.program_id(0)
    special = None
    for kind, ranges in epilogues.items():
        cond = None
        for lo, hi in ranges:
            c = jnp.logical_and(j >= lo, j < hi)
            cond = c if cond is None else jnp.logical_or(cond, c)
        pl.when(cond)(emit(kind))
        special = cond if special is None else jnp.logical_or(special, cond)
    pl.when(jnp.logical_not(special))(emit("plain"))


def _inproj(h2, w_in, layer, col_scale, rope, *, col_map, n_out, tm, tn, activations,
            side_weights=()):
    m, d = h2.shape
    n_i = m // tm
    epilogues = {}
    if activations:
        epilogues["silu"] = [(OFF_NA_Z // tn, OFF_RET_Q // tn), (OFF_RET_Z // tn, OFF_G_NA // tn)]
        epilogues["sigmoid"] = [(OFF_G_NA // tn, IN_COLS // tn)]
    in_specs = [
        pl.BlockSpec((tm, d), lambda j, i: (i, 0)),
        pl.BlockSpec((None, d, tn), lambda j, i: (layer, 0, col_map(j))),
        pl.BlockSpec((1, tn), lambda j, i: (0, col_map(j))),
    ]
    args = [h2, w_in, col_scale]
    if rope is not None:
        cosf, sinf = rope
        seq_tiles = cosf.shape[0] // tm
        in_specs += [pl.BlockSpec((tm, RET_KEY_DIM), lambda j, i: (i % seq_tiles, 0))] * 2
        args += [cosf, sinf]
        epilogues["rope"] = [(OFF_RET_Q // tn, OFF_RET_V // tn)]
    out_specs = [pl.BlockSpec((tm, tn), lambda j, i: (i, j))]
    out_shape = [jax.ShapeDtypeStruct((m, n_out), BF16)]
    for w in side_weights:
        _, rows, cols = w.shape
        last = rows // SIDE_CAST_ROWS - 1
        assert last < (n_out // tn) * n_i, "not enough grid steps to cast this weight"
        chunk = lambda j, i, last=last: jnp.minimum(j * n_i + i, last)
        in_specs.append(pl.BlockSpec((None, SIDE_CAST_ROWS, cols),
                                     lambda j, i, chunk=chunk: (layer, chunk(j, i), 0)))
        args.append(w)
        out_specs.append(pl.BlockSpec((SIDE_CAST_ROWS, cols), lambda j, i, chunk=chunk: (chunk(j, i), 0)))
        out_shape.append(jax.ShapeDtypeStruct((rows, cols), BF16))
    return pl.pallas_call(
        functools.partial(_inproj_kernel, epilogues=epilogues, n_side=len(side_weights)),
        grid=(n_out // tn, n_i),
        in_specs=in_specs,
        out_specs=out_specs,
        out_shape=out_shape,
        scratch_shapes=[pltpu.VMEM((d, tn), BF16)],
        compiler_params=pltpu.CompilerParams(
            dimension_semantics=("arbitrary", "arbitrary"), vmem_limit_bytes=VMEM_LIMIT),
        name="inproj",
    )(*args)


def _na_window_row0(r, rows):
    return min(max(r - NA_WIN_ROWS // 2, 0), rows - NA_WIN_ROWS)


def _na_key_row0(i, rows):
    return min(max(4 * i - 4, 0), rows - 12)


def _na_kernel(*refs, rows, with_ctx_q):
    if with_ctx_q:
        (pb_ref, q_ref, k_ref, v_ref, kc_ref, vc_ref, qc_ref, o_ref, oc_ref,
         bias_ref, vx_ref, vcx_ref, s_ref, p_ref) = refs
    else:
        (pb_ref, q_ref, k_ref, v_ref, kc_ref, vc_ref, o_ref,
         bias_ref, vx_ref, vcx_ref, s_ref, p_ref) = refs
    n_blk = rows // 4

    @pl.when(pl.program_id(1) == 0)
    def _():
        lane = lax.broadcasted_iota(jnp.int32, (GRID_W, 2 * GRID_W), 1)
        neg = jnp.full((GRID_W, 2 * GRID_W), MASK_VALUE, F32)
        for ti, i in enumerate((0, 1, n_blk - 1)):
            krow0 = _na_key_row0(i, rows)
            for rl in range(4):
                r = 4 * i + rl
                r0 = _na_window_row0(r, rows)
                for j in range(6):
                    ka = krow0 + 2 * j
                    in_a = r0 <= ka < r0 + NA_WIN_ROWS
                    in_b = r0 <= ka + 1 < r0 + NA_WIN_ROWS
                    if in_a or in_b:
                        t = pb_ref[0, ka - r + NA_WIN_ROWS]
                        if not in_b:
                            t = jnp.where(lane < GRID_W, t, neg)
                        if not in_a:
                            t = jnp.where(lane >= GRID_W, t, neg)
                    else:
                        t = neg
                    bias_ref[ti, rl * GRID_W:(rl + 1) * GRID_W, j * 128:(j + 1) * 128] = t

    @pl.when(jnp.logical_and(pl.program_id(0) == 0, pl.program_id(1) == 0))
    def _():
        vx_ref[:, NA_DIM:] = jnp.ones((vx_ref.shape[0], NA_DIM), BF16)
        vcx_ref[:, NA_DIM:] = jnp.ones((vcx_ref.shape[0], NA_DIM), BF16)

    vx_ref[:, 0:NA_DIM] = v_ref[0]
    vcx_ref[:, 0:NA_DIM] = vc_ref[0]
    kc = kc_ref[0]
    n_slots = s_ref.shape[0]

    def key_rows(i):
        k0 = _na_key_row0(i, rows) * GRID_W
        return slice(k0, k0 + NA_KBLK)

    def scores(i):
        q = q_ref[0, i * NA_QBLK:(i + 1) * NA_QBLK, :]
        ti = 0 if i == 0 else (2 if i == n_blk - 1 else 1)
        s_ref[i % n_slots, :, 0:NA_KBLK] = _dot_nt(q, k_ref[0, key_rows(i), :]) + bias_ref[ti]
        s_ref[i % n_slots, :, NA_KBLK:] = _dot_nt(q, kc)

    def probs(i):
        for r0 in range(0, NA_QBLK, NA_PROB_ROWS):
            s = s_ref[i % n_slots, r0:r0 + NA_PROB_ROWS, :]
            p_ref[i % n_slots, r0:r0 + NA_PROB_ROWS, :] = jnp.exp2(
                s - jnp.max(s, axis=-1, keepdims=True)).astype(BF16)

    def values(i):
        k0 = key_rows(i).start
        ksplit = (NA_KBLK + kc.shape[0]) // 2
        p = p_ref.at[i % n_slots]
        ox = (_dot(p[:, 0:ksplit], vx_ref[k0:k0 + ksplit, :])
              + (_dot(p[:, ksplit:NA_KBLK], vx_ref[k0 + ksplit:k0 + NA_KBLK, :])
                 + _dot(p[:, NA_KBLK:], vcx_ref[...])))
        o_ref[0, i * NA_QBLK:(i + 1) * NA_QBLK, :] = (
            ox[:, 0:NA_DIM] / ox[:, NA_DIM:]).astype(o_ref.dtype)

    for t in range(n_blk + 2):
        if 0 <= t - 2:
            values(t - 2)
        if 0 <= t - 1 < n_blk:
            probs(t - 1)
        if t < n_blk:
            scores(t)

    if with_ctx_q:
        sc = _dot_nt(qc_ref[0], kc)
        pc = jnp.exp2(sc - jnp.max(sc, axis=-1, keepdims=True)).astype(BF16)
        ox = _dot(pc, vcx_ref[...])
        oc_ref[0] = (ox[:, 0:NA_DIM] / ox[:, NA_DIM:]).astype(oc_ref.dtype)


def _na_attention(pb, u3, uc3, *, ctx_k_off, ctx_v_off, ctx_q_off):
    b, s, _ = u3.shape
    lc = uc3.shape[1]
    rows = s // GRID_W
    with_ctx_q = ctx_q_off is not None
    hb = lambda off: off // NA_DIM
    in_specs = [
        pl.BlockSpec((1,) + pb.shape[1:], lambda h, bb: (h, 0, 0, 0)),
        pl.BlockSpec((1, s, NA_DIM), lambda h, bb: (bb, 0, hb(OFF_NA_Q) + h)),
        pl.BlockSpec((1, s, NA_DIM), lambda h, bb: (bb, 0, hb(OFF_NA_K) + h)),
        pl.BlockSpec((1, s, NA_DIM), lambda h, bb: (bb, 0, hb(OFF_NA_V) + h)),
        pl.BlockSpec((1, lc, NA_DIM), lambda h, bb: (bb, 0, hb(ctx_k_off) + h)),
        pl.BlockSpec((1, lc, NA_DIM), lambda h, bb: (bb, 0, hb(ctx_v_off) + h)),
    ]
    args = [pb, u3, u3, u3, uc3, uc3]
    out_specs = [pl.BlockSpec((1, s, NA_DIM), lambda h, bb: (bb, 0, h))]
    out_shape = [jax.ShapeDtypeStruct((b, s, N_HEADS * NA_DIM), BF16)]
    if with_ctx_q:
        in_specs.append(pl.BlockSpec((1, lc, NA_DIM), lambda h, bb: (bb, 0, hb(ctx_q_off) + h)))
        args.append(uc3)
        out_specs.append(pl.BlockSpec((1, lc, NA_DIM), lambda h, bb: (bb, 0, h)))
        out_shape.append(jax.ShapeDtypeStruct((b, lc, N_HEADS * NA_DIM), BF16))
    outs = pl.pallas_call(
        functools.partial(_na_kernel, rows=rows, with_ctx_q=with_ctx_q),
        grid=(N_HEADS, b),
        in_specs=in_specs,
        out_specs=out_specs,
        out_shape=out_shape,
        scratch_shapes=[
            pltpu.VMEM((3, NA_QBLK, NA_KBLK), F32),
            pltpu.VMEM((s, 2 * NA_DIM), BF16),
            pltpu.VMEM((lc, 2 * NA_DIM), BF16),
            pltpu.VMEM((NA_SLOTS, NA_QBLK, NA_KBLK + lc), F32),
            pltpu.VMEM((NA_SLOTS, NA_QBLK, NA_KBLK + lc), BF16),
        ],
        compiler_params=pltpu.CompilerParams(
            dimension_semantics=("arbitrary", "arbitrary"), vmem_limit_bytes=VMEM_LIMIT),
        name="na_attention",
    )(*args)
    return outs if with_ctx_q else (outs[0], None)


def _na_pair_bias(rpb):
    cq = jnp.arange(GRID_W)[:, None]
    ck = jnp.arange(GRID_W)[None, :]
    c0 = jnp.clip(cq - NA_WIN_COLS // 2, 0, GRID_W - NA_WIN_COLS)
    col_in = (ck >= c0) & (ck < c0 + NA_WIN_COLS)
    dc = jnp.clip(ck - cq + (NA_WIN_COLS - 1), 0, 2 * NA_WIN_COLS - 2)
    onehot = ((dc[None] == jnp.arange(2 * NA_WIN_COLS - 1)[:, None, None]) & col_in[None]).astype(F32)
    cb = jnp.einsum('hrd,dqk->hrqk', rpb.astype(F32), onehot, precision=lax.Precision.HIGHEST)
    cb = jnp.where(col_in[None, None], cb * LOG2E, MASK_VALUE)
    neg = jnp.full_like(cb[:, :1], MASK_VALUE)
    cbx = jnp.concatenate([neg, cb, neg], axis=1)
    return jnp.concatenate([cbx[:, :-1], cbx[:, 1:]], axis=-1)


def _ret_kernel(*refs, with_ctx_q):
    if with_ctx_q:
        (dl_ref, q_ref, k_ref, v_ref, kc_ref, vc_ref, qc_ref,
         o_ref, oc_ref, kv_ref, st_ref) = refs
    else:
        dl_ref, q_ref, k_ref, v_ref, kc_ref, vc_ref, o_ref, kv_ref, st_ref = refs
        qc_ref = oc_ref = None
    cs = RET_CHUNK
    dk = RET_KEY_DIM

    dl = dl_ref[...]
    lg = -(jnp.maximum(-dl, 0.0) + jnp.log1p(jnp.exp(-jnp.abs(dl))))
    head = lax.broadcasted_iota(jnp.int32, dl.shape, 1)
    lgh = jnp.sum(jnp.where(head == pl.program_id(1), lg, 0.0), axis=1, keepdims=True)
    lgf = lgh[0:1, :]
    lgb = lgh[1:2, :]

    pos = lax.broadcasted_iota(jnp.int32, (cs, dk), 0).astype(F32)
    q_dec_f = jnp.exp((pos + 1.0) * lgf)
    q_dec_b = jnp.exp((cs - pos) * lgb)
    k_dec_f = jnp.exp((cs - 1.0 - pos) * lgf)
    k_dec_b = jnp.exp(pos * lgb)
    dec_f = jnp.exp(cs * lgf)
    dec_b = jnp.exp(cs * lgb)
    ii = lax.broadcasted_iota(jnp.int32, (cs, cs), 0)
    jj = lax.broadcasted_iota(jnp.int32, (cs, cs), 1)
    diff = (ii - jj).astype(F32)
    intra = jnp.where(diff > 0, jnp.exp(jnp.maximum(diff, 0.0) * lgf),
                      jnp.where(diff < 0, jnp.exp(jnp.maximum(-diff, 0.0) * lgb), 2.0))

    def run(qr, kr, vr, outr, n_tok, s0f, s0b):
        nc = n_tok // cs
        chunk = lambda t: slice(t * cs, (t + 1) * cs)

        for t in range(nc):
            sl = chunk(t)
            k = kr[0, sl, :].astype(F32)
            kk = jnp.concatenate([k * k_dec_f, k * k_dec_b], axis=1).astype(BF16)
            kv_ref[t] = _dot_tn(kk, vr[0, sl, :])
            if outr is not None:
                st_ref[t, 0:cs, :] = vr[0, sl, :]

        sf = s0f
        for t in range(nc):
            st_ref[t, cs:cs + dk, :] = sf.astype(BF16)
            sf = sf * dec_f + kv_ref[t, 0:dk, :]
        sb = s0b
        for t in reversed(range(nc)):
            st_ref[t, cs + dk:cs + 2 * dk, :] = sb.astype(BF16)
            sb = sb * dec_b + kv_ref[t, dk:2 * dk, :]

        if outr is not None:
            for t in range(nc):
                sl = chunk(t)
                qb = qr[0, sl, :]
                q = qb.astype(F32)
                a = _dot_nt(qb, kr[0, sl, :])
                lhs = jnp.concatenate([a * intra, q * q_dec_f, q * q_dec_b], axis=1).astype(BF16)
                o = _dot(lhs, st_ref[t])
                o = o * lax.rsqrt(jnp.mean(o * o, axis=-1, keepdims=True) + NORM_EPS)
                outr[0, sl, :] = o.astype(outr.dtype)
        return sf, sb

    zero = jnp.zeros((dk, RET_VAL_DIM), F32)
    sf, sb = run(qc_ref, kc_ref, vc_ref, oc_ref, kc_ref.shape[1], zero, zero)
    run(q_ref, k_ref, v_ref, o_ref, k_ref.shape[1], sf, sb)


def _retention(decay_logit, u3, uc3, *, ctx_k_off, ctx_v_off, ctx_q_off):
    b, s, _ = u3.shape
    lc = uc3.shape[1]
    with_ctx_q = ctx_q_off is not None
    kb = lambda off: off // RET_KEY_DIM
    vb = lambda off: off // RET_VAL_DIM
    in_specs = [
        pl.BlockSpec(decay_logit.shape, lambda bb, h: (0, 0)),
        pl.BlockSpec((1, s, RET_KEY_DIM), lambda bb, h: (bb, 0, kb(OFF_RET_Q) + h)),
        pl.BlockSpec((1, s, RET_KEY_DIM), lambda bb, h: (bb, 0, kb(OFF_RET_K) + h)),
        pl.BlockSpec((1, s, RET_VAL_DIM), lambda bb, h: (bb, 0, vb(OFF_RET_V) + h)),
        pl.BlockSpec((1, lc, RET_KEY_DIM), lambda bb, h: (bb, 0, kb(ctx_k_off) + h)),
        pl.BlockSpec((1, lc, RET_VAL_DIM), lambda bb, h: (bb, 0, vb(ctx_v_off) + h)),
    ]
    args = [decay_logit, u3, u3, u3, uc3, uc3]
    out_specs = [pl.BlockSpec((1, s, RET_VAL_DIM), lambda bb, h: (bb, 0, h))]
    out_shape = [jax.ShapeDtypeStruct((b, s, N_HEADS * RET_VAL_DIM), BF16)]
    if with_ctx_q:
        in_specs.append(pl.BlockSpec((1, lc, RET_KEY_DIM), lambda bb, h: (bb, 0, kb(ctx_q_off) + h)))
        args.append(uc3)
        out_specs.append(pl.BlockSpec((1, lc, RET_VAL_DIM), lambda bb, h: (bb, 0, h)))
        out_shape.append(jax.ShapeDtypeStruct((b, lc, N_HEADS * RET_VAL_DIM), BF16))
    nc = s // RET_CHUNK
    outs = pl.pallas_call(
        functools.partial(_ret_kernel, with_ctx_q=with_ctx_q),
        grid=(b, N_HEADS),
        in_specs=in_specs,
        out_specs=out_specs,
        out_shape=out_shape,
        scratch_shapes=[
            pltpu.VMEM((nc, 2 * RET_KEY_DIM, RET_VAL_DIM), F32),
            pltpu.VMEM((nc, RET_CHUNK + 2 * RET_KEY_DIM, RET_VAL_DIM), BF16),
        ],
        compiler_params=pltpu.CompilerParams(
            dimension_semantics=("arbitrary", "arbitrary"), vmem_limit_bytes=VMEM_LIMIT),
        name="retention",
    )(*args)
    return outs if with_ctx_q else (outs[0], None)


def _merge_kernel(*refs, mode):
    (ona_ref, zna_ref, oret_ref, zret_ref, gna_ref, gret_ref, x_ref, gate_ref,
     wna_ref, wret_ref, wout_ref) = refs[:11]
    a = (ona_ref[...].astype(F32) * zna_ref[...].astype(F32)).astype(BF16)
    y_na = _dot(a, wna_ref[...])
    r = (oret_ref[...].astype(F32) * zret_ref[...].astype(F32)).astype(BF16)
    y_ret = _dot(r, wret_ref[...])
    merged = (gna_ref[...].astype(F32) * y_na + gret_ref[...].astype(F32) * y_ret).astype(BF16)
    if mode == "final":
        fg_ref, xn_ref = refs[11:]
    elif mode == "stream+next":
        g_ref, scale_ref, shift_ref, xn_ref, h_ref = refs[11:]
    else:
        g_ref, scale_ref, shift_ref, h_ref, xn_ref = refs[11:]

    d = x_ref.shape[1]
    ssq = jnp.zeros((x_ref.shape[0], 128), F32)
    for c0 in range(0, d, MERGE_TN):
        cols = slice(c0, c0 + MERGE_TN)
        xn = x_ref[:, cols] + gate_ref[0, :, cols] * _dot(merged, wout_ref[:, cols])
        xn_ref[:, cols] = xn
        for l0 in range(0, MERGE_TN, 128):
            ssq = ssq + xn[:, l0:l0 + 128] * xn[:, l0:l0 + 128]
    rinv = lax.rsqrt(jnp.sum(ssq, axis=-1, keepdims=True) * (1.0 / d) + NORM_EPS)
    if mode == "final":
        xn_ref[...] = xn_ref[...] * rinv * fg_ref[...]
    else:
        gain = g_ref[...] * (1.0 + scale_ref[0])
        h_ref[...] = (xn_ref[...] * rinv * gain + shift_ref[0]).astype(h_ref.dtype)


def _merge(o_na, o_ret, u2, x2, gate, wna, wret, wout, *, mode, extra, rows_per_mod, mod_row, tm):
    m, d = x2.shape
    mod_idx = _mod_index(tm, rows_per_mod, mod_row)
    resident = lambda shape: pl.BlockSpec(shape, lambda i: (0, 0), pipeline_mode=pl.Buffered(1))
    row_tile = lambda width, col_blk: pl.BlockSpec((tm, width), lambda i: (i, col_blk))
    w_na_cols = N_HEADS * NA_DIM
    w_ret_cols = N_HEADS * RET_VAL_DIM
    in_specs = [
        row_tile(w_na_cols, 0),
        row_tile(w_na_cols, OFF_NA_Z // w_na_cols),
        row_tile(w_ret_cols, 0),
        row_tile(w_ret_cols, OFF_RET_Z // w_ret_cols),
        row_tile(d, OFF_G_NA // d),
        row_tile(d, OFF_G_RET // d),
        row_tile(d, 0),
        pl.BlockSpec((1, 1, d), mod_idx),
        resident(wna.shape),
        resident(wret.shape),
        resident(wout.shape),
    ]
    args = [o_na, u2, o_ret, u2, u2, u2, x2, gate, wna, wret, wout]
    stream = (row_tile(d, 0), jax.ShapeDtypeStruct((m, d), F32))
    nxt = (row_tile(d, 0), jax.ShapeDtypeStruct((m, d), BF16))
    if mode == "final":
        in_specs.append(pl.BlockSpec((1, d), lambda i: (0, 0)))
        args.append(extra)
        outs = [stream]
    else:
        g_next, scale_next, shift_next = extra
        in_specs += [pl.BlockSpec((1, d), lambda i: (0, 0)),
                     pl.BlockSpec((1, 1, d), mod_idx), pl.BlockSpec((1, 1, d), mod_idx)]
        args += [g_next, scale_next, shift_next]
        outs = [stream, nxt] if mode == "stream+next" else [nxt]
    return pl.pallas_call(
        functools.partial(_merge_kernel, mode=mode),
        grid=(m // tm,),
        in_specs=in_specs,
        out_specs=[o[0] for o in outs],
        out_shape=[o[1] for o in outs],
        scratch_shapes=[pltpu.VMEM((tm, d), F32)] if mode == "next" else [],
        compiler_params=pltpu.CompilerParams(
            dimension_semantics=("arbitrary",), vmem_limit_bytes=VMEM_LIMIT),
        name="merge",
    )(*args)


def _rope_tables(n_tokens):
    t = jnp.arange(n_tokens)
    row = (t // GRID_W).astype(F32)
    col = (t % GRID_W).astype(F32)
    n_freq = RET_KEY_DIM // 4
    inv_freq = ROPE_BASE ** (-jnp.arange(n_freq, dtype=F32) / n_freq)
    ang = jnp.concatenate([row[:, None] * inv_freq, col[:, None] * inv_freq], axis=-1)
    cos, sin = jnp.cos(ang), jnp.sin(ang)
    return jnp.concatenate([cos, cos], axis=-1), jnp.concatenate([-sin, sin], axis=-1)


def kernel(x, c, ctx, c_ctx, ada_w, ada_b, norm_g, w_in, na_rpb, ret_decay_logit,
           w_proj_na, w_proj_ret, w_out, final_g):
    b, s, d = x.shape
    lc = ctx.shape[1]
    depth = ada_w.shape[0]

    c_all = jnp.concatenate([c, c_ctx[None], jnp.zeros((MOD_ROWS - b - 1, d), F32)], axis=0)
    mod = _modulation(c_all, ada_w, ada_b)
    rope = _rope_tables(s)
    cols = jnp.arange(IN_COLS)
    col_scale = jnp.where((cols >= OFF_NA_Q) & (cols < OFF_NA_K), NA_DIM ** -0.5 * LOG2E,
                          jnp.where((cols >= OFF_RET_K) & (cols < OFF_RET_V), RET_KEY_DIM ** -0.5, 1.0)
                          ).astype(F32).reshape(1, IN_COLS)

    def modulation(l):
        shift = mod[l, :, 0:d].reshape(MOD_ROWS, 1, d)
        scale = mod[l, :, d:2 * d].reshape(MOD_ROWS, 1, d)
        gate = mod[l, :, 2 * d:3 * d].reshape(MOD_ROWS, 1, d)
        return norm_g[l].reshape(1, d), scale, shift, gate

    x_lat = x.reshape(b * s, d)
    x_ctx = ctx.reshape(b * lc, d)
    g, scale, shift, gate = modulation(0)
    h_lat = _prenorm(x_lat, shift, scale, g, rows_per_mod=s, mod_row=None, tm=PRENORM_TM)
    h_ctx = _prenorm(x_ctx, shift, scale, g, rows_per_mod=lc, mod_row=CTX_MOD_ROW, tm=PRENORM_TM)
    for l in range(depth):
        last = l == depth - 1
        u, wna, wret, wout = _inproj(h_lat, w_in, l, col_scale, rope, col_map=lambda j: j,
                                     n_out=IN_COLS, tm=INPROJ_TM, tn=INPROJ_TN, activations=True,
                                     side_weights=(w_proj_na, w_proj_ret, w_out))
        if last:
            (uc,) = _inproj(h_ctx, w_in, l, col_scale, None,
                            col_map=lambda j: jnp.where(j < 2, j + 1, j + 3),
                            n_out=CTX_KV_COLS, tm=INPROJ_TM, tn=INPROJ_TN, activations=False)
            offs = dict(na_k=COFF_NA_K, na_v=COFF_NA_V, ret_k=COFF_RET_K, ret_v=COFF_RET_V)
            na_q = ret_q = None
        else:
            (uc,) = _inproj(h_ctx, w_in, l, col_scale, None, col_map=lambda j: j, n_out=IN_COLS,
                            tm=INPROJ_TM, tn=INPROJ_TN, activations=True)
            offs = dict(na_k=OFF_NA_K, na_v=OFF_NA_V, ret_k=OFF_RET_K, ret_v=OFF_RET_V)
            na_q, ret_q = OFF_NA_Q, OFF_RET_Q
        u3 = u.reshape(b, s, IN_COLS)
        uc3 = uc.reshape(b, lc, uc.shape[-1])

        o_na, o_na_ctx = _na_attention(_na_pair_bias(na_rpb[l]), u3, uc3,
                                       ctx_k_off=offs['na_k'], ctx_v_off=offs['na_v'], ctx_q_off=na_q)
        o_ret, o_ret_ctx = _retention(ret_decay_logit[l], u3, uc3,
                                      ctx_k_off=offs['ret_k'], ctx_v_off=offs['ret_v'], ctx_q_off=ret_q)

        lat_args = (o_na.reshape(b * s, -1), o_ret.reshape(b * s, -1), u, x_lat, gate, wna, wret, wout)
        if last:
            (x_lat,) = _merge(*lat_args, mode="final", extra=final_g.reshape(1, d),
                              rows_per_mod=s, mod_row=None, tm=MERGE_TM)
            break
        g, scale, shift, gate_next = modulation(l + 1)
        x_lat, h_lat = _merge(*lat_args, mode="stream+next", extra=(g, scale, shift),
                              rows_per_mod=s, mod_row=None, tm=MERGE_TM)
        ctx_args = (o_na_ctx.reshape(b * lc, -1), o_ret_ctx.reshape(b * lc, -1), uc, x_ctx,
                    gate, wna, wret, wout)
        if l + 1 == depth - 1:
            (h_ctx,) = _merge(*ctx_args, mode="next", extra=(g, scale, shift),
                              rows_per_mod=lc, mod_row=CTX_MOD_ROW, tm=MERGE_TM)
        else:
            x_ctx, h_ctx = _merge(*ctx_args, mode="stream+next", extra=(g, scale, shift),
                                  rows_per_mod=lc, mod_row=CTX_MOD_ROW, tm=MERGE_TM)
        gate = gate_next
    return x_lat.reshape(b, s, d)
```

```python
import functools

import jax
import jax.numpy as jnp
from jax import lax
from jax.experimental import pallas as pl
from jax.experimental.pallas import tpu as pltpu

F32 = jnp.float32
BF16 = jnp.bfloat16

D_MODEL = 2048
GRID_W = 64
N_HEADS = 8
NA_DIM = 128
NA_WIN_ROWS = 8
NA_WIN_COLS = 16
RET_KEY_DIM = 128
RET_VAL_DIM = 256
RET_CHUNK = 256
ROPE_BASE = 10000.0
NORM_EPS = 1e-6
MASK_VALUE = -1e30
LOG2E = 1.4426950408889634

OFF_NA_Q, OFF_NA_K, OFF_NA_V, OFF_NA_Z = 0, 1024, 2048, 3072
OFF_RET_Q, OFF_RET_K, OFF_RET_V, OFF_RET_Z = 4096, 5120, 6144, 8192
OFF_G_NA, OFF_G_RET = 10240, 12288
IN_COLS = 14336
CTX_KV_COLS = 5120
COFF_NA_K, COFF_NA_V, COFF_RET_K, COFF_RET_V = 0, 1024, 2048, 3072

MOD_ROWS = 16
CTX_MOD_ROW = 8

NA_QBLK = 4 * GRID_W
NA_KBLK = 12 * GRID_W

VMEM_LIMIT = 56 * 1024 * 1024
PRENORM_TM = 1024
PRENORM_ROWS = 64
INPROJ_TM = 2048
INPROJ_TN = 1024
INPROJ_ROWS = 512
SIDE_CAST_ROWS = 64
MERGE_TM = 256
MERGE_TN = 512
NA_PROB_ROWS = 32
NA_SLOTS = 2


def _dot(a, b):
    return jnp.dot(a, b, preferred_element_type=F32)


def _dot_nt(a, b):
    return lax.dot_general(a, b, (((1,), (1,)), ((), ())), preferred_element_type=F32)


def _dot_tn(a, b):
    return lax.dot_general(a, b, (((0,), (0,)), ((), ())), preferred_element_type=F32)


def _sigmoid(x):
    return 0.5 * jnp.tanh(0.5 * x) + 0.5


def _silu(x):
    return x * _sigmoid(x)


def _mod_kernel(c_ref, w_ref, b_ref, o_ref):
    c = c_ref[...]
    cs = (c * _sigmoid(c)).astype(BF16)
    o_ref[0] = _dot(cs, w_ref[0].astype(BF16)) + b_ref[0]


def _modulation(c_all, ada_w, ada_b, tn=1024):
    depth, d, n = ada_w.shape
    return pl.pallas_call(
        _mod_kernel,
        grid=(depth, n // tn),
        in_specs=[
            pl.BlockSpec((MOD_ROWS, d), lambda l, j: (0, 0)),
            pl.BlockSpec((1, d, tn), lambda l, j: (l, 0, j)),
            pl.BlockSpec((1, 1, tn), lambda l, j: (l, 0, j)),
        ],
        out_specs=pl.BlockSpec((1, MOD_ROWS, tn), lambda l, j: (l, 0, j)),
        out_shape=jax.ShapeDtypeStruct((depth, MOD_ROWS, n), F32),
        compiler_params=pltpu.CompilerParams(
            dimension_semantics=("arbitrary", "arbitrary"), vmem_limit_bytes=VMEM_LIMIT),
        name="adaln_mod",
    )(c_all, ada_w, ada_b.reshape(depth, 1, n))


def _prenorm_kernel(x_ref, shift_ref, scale_ref, g_ref, o_ref, rinv_ref):
    n_chunks = x_ref.shape[0] // PRENORM_ROWS

    def rows(r):
        return pl.ds(pl.multiple_of(r * PRENORM_ROWS, PRENORM_ROWS), PRENORM_ROWS)

    def stats(r, carry):
        x = x_ref[rows(r), :]
        rinv_ref[rows(r), :] = lax.rsqrt(jnp.mean(x * x, axis=-1, keepdims=True) + NORM_EPS)
        return carry
    lax.fori_loop(0, n_chunks, stats, 0, unroll=2)

    gain = g_ref[...] * (1.0 + scale_ref[0])
    shift = shift_ref[0]

    def apply(r, carry):
        o_ref[rows(r), :] = (x_ref[rows(r), :] * rinv_ref[rows(r), :] * gain + shift).astype(o_ref.dtype)
        return carry
    lax.fori_loop(0, n_chunks, apply, 0, unroll=2)


def _mod_index(tm, rows_per_mod, mod_row):
    if mod_row is None:
        return lambda i: ((i * tm) // rows_per_mod, 0, 0)
    return lambda i: (mod_row, 0, 0)


def _prenorm(x2, shift, scale, g, *, rows_per_mod, mod_row, tm):
    m, d = x2.shape
    mod_idx = _mod_index(tm, rows_per_mod, mod_row)
    return pl.pallas_call(
        _prenorm_kernel,
        grid=(m // tm,),
        in_specs=[
            pl.BlockSpec((tm, d), lambda i: (i, 0)),
            pl.BlockSpec((1, 1, d), mod_idx),
            pl.BlockSpec((1, 1, d), mod_idx),
            pl.BlockSpec((1, d), lambda i: (0, 0)),
        ],
        out_specs=pl.BlockSpec((tm, d), lambda i: (i, 0)),
        out_shape=jax.ShapeDtypeStruct((m, d), BF16),
        scratch_shapes=[pltpu.VMEM((tm, 1), F32)],
        compiler_params=pltpu.CompilerParams(
            dimension_semantics=("arbitrary",), vmem_limit_bytes=VMEM_LIMIT),
        name="prenorm",
    )(x2, shift, scale, g)


def _inproj_kernel(*refs, rope_tiles, n_side):
    n_in = 3 + (0 if rope_tiles is None else 2)
    side_in = refs[n_in:n_in + n_side]
    side_out = refs[n_in + n_side + 1:n_in + 2 * n_side + 1]
    for src, dst in zip(side_in, side_out):
        dst[...] = src[...].astype(dst.dtype)
    refs = refs[:n_in] + (refs[n_in + n_side],) + refs[n_in + 2 * n_side + 1:]

    if rope_tiles is None:
        h_ref, w_ref, cs_ref, o_ref, wb_ref = refs
    else:
        h_ref, w_ref, cs_ref, cos_ref, sin_ref, o_ref, wb_ref = refs

    @pl.when(pl.program_id(1) == 0)
    def _():
        wb_ref[...] = w_ref[...].astype(BF16)

    row_chunks = [slice(r0, r0 + INPROJ_ROWS) for r0 in range(0, h_ref.shape[0], INPROJ_ROWS)]

    def project(rows):
        return _dot(h_ref[rows, :], wb_ref[...]) * cs_ref[...]

    def plain():
        for rows in row_chunks:
            o_ref[rows, :] = project(rows).astype(o_ref.dtype)

    if rope_tiles is None:
        plain()
        return

    j = pl.program_id(0)
    is_rope = jnp.logical_and(j >= rope_tiles[0], j < rope_tiles[1])
    pl.when(jnp.logical_not(is_rope))(plain)

    @pl.when(is_rope)
    def _():
        for rows in row_chunks:
            acc = project(rows)
            cos, sin = cos_ref[rows, :], sin_ref[rows, :]
            for c0 in range(0, acc.shape[1], RET_KEY_DIM):
                a = acc[:, c0:c0 + RET_KEY_DIM]
                o_ref[rows, c0:c0 + RET_KEY_DIM] = (
                    a * cos + pltpu.roll(a, RET_KEY_DIM // 2, 1) * sin).astype(o_ref.dtype)


def _inproj(h2, w_in, layer, col_scale, rope, *, col_map, n_out, tm, tn, side_weights=()):
    m, d = h2.shape
    n_i = m // tm
    in_specs = [
        pl.BlockSpec((tm, d), lambda j, i: (i, 0)),
        pl.BlockSpec((None, d, tn), lambda j, i: (layer, 0, col_map(j))),
        pl.BlockSpec((1, tn), lambda j, i: (0, col_map(j))),
    ]
    args = [h2, w_in, col_scale]
    rope_tiles = None
    if rope is not None:
        cosf, sinf = rope
        seq_tiles = cosf.shape[0] // tm
        in_specs += [pl.BlockSpec((tm, RET_KEY_DIM), lambda j, i: (i % seq_tiles, 0))] * 2
        args += [cosf, sinf]
        rope_tiles = (OFF_RET_Q // tn, OFF_RET_V // tn)
    out_specs = [pl.BlockSpec((tm, tn), lambda j, i: (i, j))]
    out_shape = [jax.ShapeDtypeStruct((m, n_out), BF16)]
    for w in side_weights:
        _, rows, cols = w.shape
        last = rows // SIDE_CAST_ROWS - 1
        assert last < (n_out // tn) * n_i, "not enough grid steps to cast this weight"
        chunk = lambda j, i, last=last: jnp.minimum(j * n_i + i, last)
        in_specs.append(pl.BlockSpec((None, SIDE_CAST_ROWS, cols),
                                     lambda j, i, chunk=chunk: (layer, chunk(j, i), 0)))
        args.append(w)
        out_specs.append(pl.BlockSpec((SIDE_CAST_ROWS, cols), lambda j, i, chunk=chunk: (chunk(j, i), 0)))
        out_shape.append(jax.ShapeDtypeStruct((rows, cols), BF16))
    return pl.pallas_call(
        functools.partial(_inproj_kernel, rope_tiles=rope_tiles, n_side=len(side_weights)),
        grid=(n_out // tn, n_i),
        in_specs=in_specs,
        out_specs=out_specs,
        out_shape=out_shape,
        scratch_shapes=[pltpu.VMEM((d, tn), BF16)],
        compiler_params=pltpu.CompilerParams(
            dimension_semantics=("arbitrary", "arbitrary"), vmem_limit_bytes=VMEM_LIMIT),
        name="inproj",
    )(*args)


def _na_window_row0(r, rows):
    return min(max(r - NA_WIN_ROWS // 2, 0), rows - NA_WIN_ROWS)


def _na_key_row0(i, rows):
    return min(max(4 * i - 4, 0), rows - 12)


def _na_kernel(*refs, rows, with_ctx_q):
    if with_ctx_q:
        (pb_ref, q_ref, k_ref, v_ref, kc_ref, vc_ref, qc_ref, o_ref, oc_ref,
         bias_ref, vx_ref, vcx_ref, s_ref, p_ref) = refs
    else:
        (pb_ref, q_ref, k_ref, v_ref, kc_ref, vc_ref, o_ref,
         bias_ref, vx_ref, vcx_ref, s_ref, p_ref) = refs
    n_blk = rows // 4

    @pl.when(pl.program_id(1) == 0)
    def _():
        lane = lax.broadcasted_iota(jnp.int32, (GRID_W, 2 * GRID_W), 1)
        neg = jnp.full((GRID_W, 2 * GRID_W), MASK_VALUE, F32)
        for ti, i in enumerate((0, 1, n_blk - 1)):
            krow0 = _na_key_row0(i, rows)
            for rl in range(4):
                r = 4 * i + rl
                r0 = _na_window_row0(r, rows)
                for j in range(6):
                    ka = krow0 + 2 * j
                    in_a = r0 <= ka < r0 + NA_WIN_ROWS
                    in_b = r0 <= ka + 1 < r0 + NA_WIN_ROWS
                    if in_a or in_b:
                        t = pb_ref[0, ka - r + NA_WIN_ROWS]
                        if not in_b:
                            t = jnp.where(lane < GRID_W, t, neg)
                        if not in_a:
                            t = jnp.where(lane >= GRID_W, t, neg)
                    else:
                        t = neg
                    bias_ref[ti, rl * GRID_W:(rl + 1) * GRID_W, j * 128:(j + 1) * 128] = t

    vx_ref[:, 0:NA_DIM] = v_ref[0]
    vx_ref[:, NA_DIM:] = jnp.ones((vx_ref.shape[0], NA_DIM), BF16)
    vcx_ref[:, 0:NA_DIM] = vc_ref[0]
    vcx_ref[:, NA_DIM:] = jnp.ones((vcx_ref.shape[0], NA_DIM), BF16)
    kc = kc_ref[0]
    n_slots = s_ref.shape[0]

    def key_rows(i):
        k0 = _na_key_row0(i, rows) * GRID_W
        return slice(k0, k0 + NA_KBLK)

    def scores(i):
        q = q_ref[0, i * NA_QBLK:(i + 1) * NA_QBLK, :]
        ti = 0 if i == 0 else (2 if i == n_blk - 1 else 1)
        s_ref[i % n_slots, :, 0:NA_KBLK] = _dot_nt(q, k_ref[0, key_rows(i), :]) + bias_ref[ti]
        s_ref[i % n_slots, :, NA_KBLK:] = _dot_nt(q, kc)

    def probs(i):
        for r0 in range(0, NA_QBLK, NA_PROB_ROWS):
            s = s_ref[i % n_slots, r0:r0 + NA_PROB_ROWS, :]
            p_ref[i % n_slots, r0:r0 + NA_PROB_ROWS, :] = jnp.exp2(
                s - jnp.max(s, axis=-1, keepdims=True)).astype(BF16)

    def values(i):
        k0 = key_rows(i).start
        ksplit = (NA_KBLK + kc.shape[0]) // 2
        p = p_ref.at[i % n_slots]
        ox = (_dot(p[:, 0:ksplit], vx_ref[k0:k0 + ksplit, :])
              + (_dot(p[:, ksplit:NA_KBLK], vx_ref[k0 + ksplit:k0 + NA_KBLK, :])
                 + _dot(p[:, NA_KBLK:], vcx_ref[...])))
        o_ref[0, i * NA_QBLK:(i + 1) * NA_QBLK, :] = (
            ox[:, 0:NA_DIM] / ox[:, NA_DIM:]).astype(o_ref.dtype)

    for t in range(n_blk + 2):
        if 0 <= t - 2:
            values(t - 2)
        if 0 <= t - 1 < n_blk:
            probs(t - 1)
        if t < n_blk:
            scores(t)

    if with_ctx_q:
        sc = _dot_nt(qc_ref[0], kc)
        pc = jnp.exp2(sc - jnp.max(sc, axis=-1, keepdims=True)).astype(BF16)
        ox = _dot(pc, vcx_ref[...])
        oc_ref[0] = (ox[:, 0:NA_DIM] / ox[:, NA_DIM:]).astype(oc_ref.dtype)


def _na_attention(pb, u3, uc3, *, ctx_k_off, ctx_v_off, ctx_q_off):
    b, s, _ = u3.shape
    lc = uc3.shape[1]
    rows = s // GRID_W
    with_ctx_q = ctx_q_off is not None
    hb = lambda off: off // NA_DIM
    in_specs = [
        pl.BlockSpec((1,) + pb.shape[1:], lambda h, bb: (h, 0, 0, 0)),
        pl.BlockSpec((1, s, NA_DIM), lambda h, bb: (bb, 0, hb(OFF_NA_Q) + h)),
        pl.BlockSpec((1, s, NA_DIM), lambda h, bb: (bb, 0, hb(OFF_NA_K) + h)),
        pl.BlockSpec((1, s, NA_DIM), lambda h, bb: (bb, 0, hb(OFF_NA_V) + h)),
        pl.BlockSpec((1, lc, NA_DIM), lambda h, bb: (bb, 0, hb(ctx_k_off) + h)),
        pl.BlockSpec((1, lc, NA_DIM), lambda h, bb: (bb, 0, hb(ctx_v_off) + h)),
    ]
    args = [pb, u3, u3, u3, uc3, uc3]
    out_specs = [pl.BlockSpec((1, s, NA_DIM), lambda h, bb: (bb, 0, h))]
    out_shape = [jax.ShapeDtypeStruct((b, s, N_HEADS * NA_DIM), BF16)]
    if with_ctx_q:
        in_specs.append(pl.BlockSpec((1, lc, NA_DIM), lambda h, bb: (bb, 0, hb(ctx_q_off) + h)))
        args.append(uc3)
        out_specs.append(pl.BlockSpec((1, lc, NA_DIM), lambda h, bb: (bb, 0, h)))
        out_shape.append(jax.ShapeDtypeStruct((b, lc, N_HEADS * NA_DIM), BF16))
    outs = pl.pallas_call(
        functools.partial(_na_kernel, rows=rows, with_ctx_q=with_ctx_q),
        grid=(N_HEADS, b),
        in_specs=in_specs,
        out_specs=out_specs,
        out_shape=out_shape,
        scratch_shapes=[
            pltpu.VMEM((3, NA_QBLK, NA_KBLK), F32),
            pltpu.VMEM((s, 2 * NA_DIM), BF16),
            pltpu.VMEM((lc, 2 * NA_DIM), BF16),
            pltpu.VMEM((NA_SLOTS, NA_QBLK, NA_KBLK + lc), F32),
            pltpu.VMEM((NA_SLOTS, NA_QBLK, NA_KBLK + lc), BF16),
        ],
        compiler_params=pltpu.CompilerParams(
            dimension_semantics=("arbitrary", "arbitrary"), vmem_limit_bytes=VMEM_LIMIT),
        name="na_attention",
    )(*args)
    return outs if with_ctx_q else (outs[0], None)


def _na_pair_bias(rpb):
    cq = jnp.arange(GRID_W)[:, None]
    ck = jnp.arange(GRID_W)[None, :]
    c0 = jnp.clip(cq - NA_WIN_COLS // 2, 0, GRID_W - NA_WIN_COLS)
    col_in = (ck >= c0) & (ck < c0 + NA_WIN_COLS)
    dc = jnp.clip(ck - cq + (NA_WIN_COLS - 1), 0, 2 * NA_WIN_COLS - 2)
    onehot = ((dc[None] == jnp.arange(2 * NA_WIN_COLS - 1)[:, None, None]) & col_in[None]).astype(F32)
    cb = jnp.einsum('hrd,dqk->hrqk', rpb.astype(F32), onehot, precision=lax.Precision.HIGHEST)
    cb = jnp.where(col_in[None, None], cb * LOG2E, MASK_VALUE)
    neg = jnp.full_like(cb[:, :1], MASK_VALUE)
    cbx = jnp.concatenate([neg, cb, neg], axis=1)
    return jnp.concatenate([cbx[:, :-1], cbx[:, 1:]], axis=-1)


def _ret_kernel(*refs, with_ctx_q):
    if with_ctx_q:
        (dl_ref, q_ref, k_ref, v_ref, kc_ref, vc_ref, qc_ref,
         o_ref, oc_ref, kv_ref, st_ref) = refs
    else:
        dl_ref, q_ref, k_ref, v_ref, kc_ref, vc_ref, o_ref, kv_ref, st_ref = refs
        qc_ref = oc_ref = None
    cs = RET_CHUNK
    dk = RET_KEY_DIM

    dl = dl_ref[...]
    lg = -(jnp.maximum(-dl, 0.0) + jnp.log1p(jnp.exp(-jnp.abs(dl))))
    head = lax.broadcasted_iota(jnp.int32, dl.shape, 1)
    lgh = jnp.sum(jnp.where(head == pl.program_id(1), lg, 0.0), axis=1, keepdims=True)
    lgf = lgh[0:1, :]
    lgb = lgh[1:2, :]

    pos = lax.broadcasted_iota(jnp.int32, (cs, dk), 0).astype(F32)
    q_dec_f = jnp.exp((pos + 1.0) * lgf)
    q_dec_b = jnp.exp((cs - pos) * lgb)
    k_dec_f = jnp.exp((cs - 1.0 - pos) * lgf)
    k_dec_b = jnp.exp(pos * lgb)
    dec_f = jnp.exp(cs * lgf)
    dec_b = jnp.exp(cs * lgb)
    ii = lax.broadcasted_iota(jnp.int32, (cs, cs), 0)
    jj = lax.broadcasted_iota(jnp.int32, (cs, cs), 1)
    diff = (ii - jj).astype(F32)
    intra = jnp.where(diff > 0, jnp.exp(jnp.maximum(diff, 0.0) * lgf),
                      jnp.where(diff < 0, jnp.exp(jnp.maximum(-diff, 0.0) * lgb), 2.0))

    def run(qr, kr, vr, outr, n_tok, s0f, s0b):
        nc = n_tok // cs
        chunk = lambda t: slice(t * cs, (t + 1) * cs)

        for t in range(nc):
            sl = chunk(t)
            k = kr[0, sl, :].astype(F32)
            kk = jnp.concatenate([k * k_dec_f, k * k_dec_b], axis=1).astype(BF16)
            kv_ref[t] = _dot_tn(kk, vr[0, sl, :])
            if outr is not None:
                st_ref[t, 0:cs, :] = vr[0, sl, :]

        sf = s0f
        for t in range(nc):
            st_ref[t, cs:cs + dk, :] = sf.astype(BF16)
            sf = sf * dec_f + kv_ref[t, 0:dk, :]
        sb = s0b
        for t in reversed(range(nc)):
            st_ref[t, cs + dk:cs + 2 * dk, :] = sb.astype(BF16)
            sb = sb * dec_b + kv_ref[t, dk:2 * dk, :]

        if outr is not None:
            for t in range(nc):
                sl = chunk(t)
                qb = qr[0, sl, :]
                q = qb.astype(F32)
                a = _dot_nt(qb, kr[0, sl, :])
                lhs = jnp.concatenate([a * intra, q * q_dec_f, q * q_dec_b], axis=1).astype(BF16)
                o = _dot(lhs[:, 0:cs], st_ref[t, 0:cs, :]) + _dot(lhs[:, cs:], st_ref[t, cs:, :])
                o = o * lax.rsqrt(jnp.mean(o * o, axis=-1, keepdims=True) + NORM_EPS)
                outr[0, sl, :] = o.astype(outr.dtype)
        return sf, sb

    zero = jnp.zeros((dk, RET_VAL_DIM), F32)
    sf, sb = run(qc_ref, kc_ref, vc_ref, oc_ref, kc_ref.shape[1], zero, zero)
    run(q_ref, k_ref, v_ref, o_ref, k_ref.shape[1], sf, sb)


def _retention(decay_logit, u3, uc3, *, ctx_k_off, ctx_v_off, ctx_q_off):
    b, s, _ = u3.shape
    lc = uc3.shape[1]
    with_ctx_q = ctx_q_off is not None
    kb = lambda off: off // RET_KEY_DIM
    vb = lambda off: off // RET_VAL_DIM
    in_specs = [
        pl.BlockSpec(decay_logit.shape, lambda bb, h: (0, 0)),
        pl.BlockSpec((1, s, RET_KEY_DIM), lambda bb, h: (bb, 0, kb(OFF_RET_Q) + h)),
        pl.BlockSpec((1, s, RET_KEY_DIM), lambda bb, h: (bb, 0, kb(OFF_RET_K) + h)),
        pl.BlockSpec((1, s, RET_VAL_DIM), lambda bb, h: (bb, 0, vb(OFF_RET_V) + h)),
        pl.BlockSpec((1, lc, RET_KEY_DIM), lambda bb, h: (bb, 0, kb(ctx_k_off) + h)),
        pl.BlockSpec((1, lc, RET_VAL_DIM), lambda bb, h: (bb, 0, vb(ctx_v_off) + h)),
    ]
    args = [decay_logit, u3, u3, u3, uc3, uc3]
    out_specs = [pl.BlockSpec((1, s, RET_VAL_DIM), lambda bb, h: (bb, 0, h))]
    out_shape = [jax.ShapeDtypeStruct((b, s, N_HEADS * RET_VAL_DIM), BF16)]
    if with_ctx_q:
        in_specs.append(pl.BlockSpec((1, lc, RET_KEY_DIM), lambda bb, h: (bb, 0, kb(ctx_q_off) + h)))
        args.append(uc3)
        out_specs.append(pl.BlockSpec((1, lc, RET_VAL_DIM), lambda bb, h: (bb, 0, h)))
        out_shape.append(jax.ShapeDtypeStruct((b, lc, N_HEADS * RET_VAL_DIM), BF16))
    nc = s // RET_CHUNK
    outs = pl.pallas_call(
        functools.partial(_ret_kernel, with_ctx_q=with_ctx_q),
        grid=(b, N_HEADS),
        in_specs=in_specs,
        out_specs=out_specs,
        out_shape=out_shape,
        scratch_shapes=[
            pltpu.VMEM((nc, 2 * RET_KEY_DIM, RET_VAL_DIM), F32),
            pltpu.VMEM((nc, RET_CHUNK + 2 * RET_KEY_DIM, RET_VAL_DIM), BF16),
        ],
        compiler_params=pltpu.CompilerParams(
            dimension_semantics=("arbitrary", "arbitrary"), vmem_limit_bytes=VMEM_LIMIT),
        name="retention",
    )(*args)
    return outs if with_ctx_q else (outs[0], None)


def _merge_kernel(*refs, mode):
    (ona_ref, zna_ref, oret_ref, zret_ref, gna_ref, gret_ref, x_ref, gate_ref,
     wna_ref, wret_ref, wout_ref) = refs[:11]
    a = (ona_ref[...].astype(F32) * _silu(zna_ref[...].astype(F32))).astype(BF16)
    y_na = _dot(a, wna_ref[...])
    r = (oret_ref[...].astype(F32) * _silu(zret_ref[...].astype(F32))).astype(BF16)
    y_ret = _dot(r, wret_ref[...])
    merged = (_sigmoid(gna_ref[...].astype(F32)) * y_na
              + _sigmoid(gret_ref[...].astype(F32)) * y_ret).astype(BF16)
    if mode == "final":
        fg_ref, xn_ref = refs[11:]
    elif mode == "stream+next":
        g_ref, scale_ref, shift_ref, xn_ref, h_ref = refs[11:]
    else:
        g_ref, scale_ref, shift_ref, h_ref, xn_ref = refs[11:]

    d = x_ref.shape[1]
    ssq = jnp.zeros((x_ref.shape[0], 128), F32)
    for c0 in range(0, d, MERGE_TN):
        cols = slice(c0, c0 + MERGE_TN)
        xn = x_ref[:, cols] + gate_ref[0, :, cols] * _dot(merged, wout_ref[:, cols])
        xn_ref[:, cols] = xn
        for l0 in range(0, MERGE_TN, 128):
            ssq = ssq + xn[:, l0:l0 + 128] * xn[:, l0:l0 + 128]
    rinv = lax.rsqrt(jnp.sum(ssq, axis=-1, keepdims=True) * (1.0 / d) + NORM_EPS)
    if mode == "final":
        xn_ref[...] = xn_ref[...] * rinv * fg_ref[...]
    else:
        gain = g_ref[...] * (1.0 + scale_ref[0])
        h_ref[...] = (xn_ref[...] * rinv * gain + shift_ref[0]).astype(h_ref.dtype)


def _merge(o_na, o_ret, u2, x2, gate, wna, wret, wout, *, mode, extra, rows_per_mod, mod_row, tm):
    m, d = x2.shape
    mod_idx = _mod_index(tm, rows_per_mod, mod_row)
    resident = lambda shape: pl.BlockSpec(shape, lambda i: (0, 0), pipeline_mode=pl.Buffered(1))
    row_tile = lambda width, col_blk: pl.BlockSpec((tm, width), lambda i: (i, col_blk))
    w_na_cols = N_HEADS * NA_DIM
    w_ret_cols = N_HEADS * RET_VAL_DIM
    in_specs = [
        row_tile(w_na_cols, 0),
        row_tile(w_na_cols, OFF_NA_Z // w_na_cols),
        row_tile(w_ret_cols, 0),
        row_tile(w_ret_cols, OFF_RET_Z // w_ret_cols),
        row_tile(d, OFF_G_NA // d),
        row_tile(d, OFF_G_RET // d),
        row_tile(d, 0),
        pl.BlockSpec((1, 1, d), mod_idx),
        resident(wna.shape),
        resident(wret.shape),
        resident(wout.shape),
    ]
    args = [o_na, u2, o_ret, u2, u2, u2, x2, gate, wna, wret, wout]
    stream = (row_tile(d, 0), jax.ShapeDtypeStruct((m, d), F32))
    nxt = (row_tile(d, 0), jax.ShapeDtypeStruct((m, d), BF16))
    if mode == "final":
        in_specs.append(pl.BlockSpec((1, d), lambda i: (0, 0)))
        args.append(extra)
        outs = [stream]
    else:
        g_next, scale_next, shift_next = extra
        in_specs += [pl.BlockSpec((1, d), lambda i: (0, 0)),
                     pl.BlockSpec((1, 1, d), mod_idx), pl.BlockSpec((1, 1, d), mod_idx)]
        args += [g_next, scale_next, shift_next]
        outs = [stream, nxt] if mode == "stream+next" else [nxt]
    return pl.pallas_call(
        functools.partial(_merge_kernel, mode=mode),
        grid=(m // tm,),
        in_specs=in_specs,
        out_specs=[o[0] for o in outs],
        out_shape=[o[1] for o in outs],
        scratch_shapes=[pltpu.VMEM((tm, d), F32)] if mode == "next" else [],
        compiler_params=pltpu.CompilerParams(
            dimension_semantics=("arbitrary",), vmem_limit_bytes=VMEM_LIMIT),
        name="merge",
    )(*args)


def _rope_tables(n_tokens):
    t = jnp.arange(n_tokens)
    row = (t // GRID_W).astype(F32)
    col = (t % GRID_W).astype(F32)
    n_freq = RET_KEY_DIM // 4
    inv_freq = ROPE_BASE ** (-jnp.arange(n_freq, dtype=F32) / n_freq)
    ang = jnp.concatenate([row[:, None] * inv_freq, col[:, None] * inv_freq], axis=-1)
    cos, sin = jnp.cos(ang), jnp.sin(ang)
    return jnp.concatenate([cos, cos], axis=-1), jnp.concatenate([-sin, sin], axis=-1)


def kernel(x, c, ctx, c_ctx, ada_w, ada_b, norm_g, w_in, na_rpb, ret_decay_logit,
           w_proj_na, w_proj_ret, w_out, final_g):
    b, s, d = x.shape
    lc = ctx.shape[1]
    depth = ada_w.shape[0]

    c_all = jnp.concatenate([c, c_ctx[None], jnp.zeros((MOD_ROWS - b - 1, d), F32)], axis=0)
    mod = _modulation(c_all, ada_w, ada_b)
    rope = _rope_tables(s)
    cols = jnp.arange(IN_COLS)
    col_scale = jnp.where((cols >= OFF_NA_Q) & (cols < OFF_NA_K), NA_DIM ** -0.5 * LOG2E,
                          jnp.where((cols >= OFF_RET_K) & (cols < OFF_RET_V), RET_KEY_DIM ** -0.5, 1.0)
                          ).astype(F32).reshape(1, IN_COLS)

    def modulation(l):
        shift = mod[l, :, 0:d].reshape(MOD_ROWS, 1, d)
        scale = mod[l, :, d:2 * d].reshape(MOD_ROWS, 1, d)
        gate = mod[l, :, 2 * d:3 * d].reshape(MOD_ROWS, 1, d)
        return norm_g[l].reshape(1, d), scale, shift, gate

    x_lat = x.reshape(b * s, d)
    x_ctx = ctx.reshape(b * lc, d)
    g, scale, shift, gate = modulation(0)
    h_lat = _prenorm(x_lat, shift, scale, g, rows_per_mod=s, mod_row=None, tm=PRENORM_TM)
    h_ctx = _prenorm(x_ctx, shift, scale, g, rows_per_mod=lc, mod_row=CTX_MOD_ROW, tm=PRENORM_TM)
    for l in range(depth):
        last = l == depth - 1
        u, wna, wret, wout = _inproj(h_lat, w_in, l, col_scale, rope, col_map=lambda j: j,
                                     n_out=IN_COLS, tm=INPROJ_TM, tn=INPROJ_TN,
                                     side_weights=(w_proj_na, w_proj_ret, w_out))
        if last:
            (uc,) = _inproj(h_ctx, w_in, l, col_scale, None,
                            col_map=lambda j: jnp.where(j < 2, j + 1, j + 3),
                            n_out=CTX_KV_COLS, tm=INPROJ_TM, tn=INPROJ_TN)
            offs = dict(na_k=COFF_NA_K, na_v=COFF_NA_V, ret_k=COFF_RET_K, ret_v=COFF_RET_V)
            na_q = ret_q = None
        else:
            (uc,) = _inproj(h_ctx, w_in, l, col_scale, None, col_map=lambda j: j, n_out=IN_COLS,
                            tm=INPROJ_TM, tn=INPROJ_TN)
            offs = dict(na_k=OFF_NA_K, na_v=OFF_NA_V, ret_k=OFF_RET_K, ret_v=OFF_RET_V)
            na_q, ret_q = OFF_NA_Q, OFF_RET_Q
        u3 = u.reshape(b, s, IN_COLS)
        uc3 = uc.reshape(b, lc, uc.shape[-1])

        o_na, o_na_ctx = _na_attention(_na_pair_bias(na_rpb[l]), u3, uc3,
                                       ctx_k_off=offs['na_k'], ctx_v_off=offs['na_v'], ctx_q_off=na_q)
        o_ret, o_ret_ctx = _retention(ret_decay_logit[l], u3, uc3,
                                      ctx_k_off=offs['ret_k'], ctx_v_off=offs['ret_v'], ctx_q_off=ret_q)

        lat_args = (o_na.reshape(b * s, -1), o_ret.reshape(b * s, -1), u, x_lat, gate, wna, wret, wout)
        if last:
            (x_lat,) = _merge(*lat_args, mode="final", extra=final_g.reshape(1, d),
                              rows_per_mod=s, mod_row=None, tm=MERGE_TM)
            break
        g, scale, shift, gate_next = modulation(l + 1)
        x_lat, h_lat = _merge(*lat_args, mode="stream+next", extra=(g, scale, shift),
                              rows_per_mod=s, mod_row=None, tm=MERGE_TM)
        ctx_args = (o_na_ctx.reshape(b * lc, -1), o_ret_ctx.reshape(b * lc, -1), uc, x_ctx,
                    gate, wna, wret, wout)
        if l + 1 == depth - 1:
            (h_ctx,) = _merge(*ctx_args, mode="next", extra=(g, scale, shift),
                              rows_per_mod=lc, mod_row=CTX_MOD_ROW, tm=MERGE_TM)
        else:
            x_ctx, h_ctx = _merge(*ctx_args, mode="stream+next", extra=(g, scale, shift),
                                  rows_per_mod=lc, mod_row=CTX_MOD_ROW, tm=MERGE_TM)
        gate = gate_next
    return x_lat.reshape(b, s, d)
```

```python
import functools

import jax
import jax.numpy as jnp
from jax import lax
from jax.experimental import pallas as pl
from jax.experimental.pallas import tpu as pltpu

F32 = jnp.float32
BF16 = jnp.bfloat16

D_MODEL = 2048
GRID_W = 64
N_HEADS = 8
NA_DIM = 128
NA_WIN_ROWS = 8
NA_WIN_COLS = 16
RET_KEY_DIM = 128
RET_VAL_DIM = 256
RET_CHUNK = 256
ROPE_BASE = 10000.0
NORM_EPS = 1e-6
MASK_VALUE = -1e30
LOG2E = 1.4426950408889634

OFF_NA_Q, OFF_NA_K, OFF_NA_V, OFF_NA_Z = 0, 1024, 2048, 3072
OFF_RET_Q, OFF_RET_K, OFF_RET_V, OFF_RET_Z = 4096, 5120, 6144, 8192
OFF_G_NA, OFF_G_RET = 10240, 12288
IN_COLS = 14336
CTX_KV_COLS = 5120
COFF_NA_K, COFF_NA_V, COFF_RET_K, COFF_RET_V = 0, 1024, 2048, 3072

MOD_ROWS = 16
CTX_MOD_ROW = 8

NA_QBLK = 4 * GRID_W
NA_KBLK = 12 * GRID_W

VMEM_LIMIT = 56 * 1024 * 1024
PRENORM_TM = 1024
PRENORM_ROWS = 64
INPROJ_TM = 2048
INPROJ_TN = 1024
INPROJ_ROWS = 512
SIDE_CAST_ROWS = 64
MERGE_TM = 256
MERGE_TN = 512
NA_PROB_ROWS = 32
NA_SLOTS = 2
BATCH_PER_STEP = 4


def _dot(a, b):
    return jnp.dot(a, b, preferred_element_type=F32)


def _dot_nt(a, b):
    return lax.dot_general(a, b, (((1,), (1,)), ((), ())), preferred_element_type=F32)


def _dot_tn(a, b):
    return lax.dot_general(a, b, (((0,), (0,)), ((), ())), preferred_element_type=F32)


def _sigmoid(x):
    return 0.5 * jnp.tanh(0.5 * x) + 0.5


def _silu(x):
    return x * _sigmoid(x)


def _mod_kernel(c_ref, w_ref, b_ref, o_ref):
    c = c_ref[...]
    cs = (c * _sigmoid(c)).astype(BF16)
    o_ref[0] = _dot(cs, w_ref[0].astype(BF16)) + b_ref[0]


def _modulation(c_all, ada_w, ada_b, tn=1024):
    depth, d, n = ada_w.shape
    return pl.pallas_call(
        _mod_kernel,
        grid=(depth, n // tn),
        in_specs=[
            pl.BlockSpec((MOD_ROWS, d), lambda l, j: (0, 0)),
            pl.BlockSpec((1, d, tn), lambda l, j: (l, 0, j)),
            pl.BlockSpec((1, 1, tn), lambda l, j: (l, 0, j)),
        ],
        out_specs=pl.BlockSpec((1, MOD_ROWS, tn), lambda l, j: (l, 0, j)),
        out_shape=jax.ShapeDtypeStruct((depth, MOD_ROWS, n), F32),
        compiler_params=pltpu.CompilerParams(
            dimension_semantics=("arbitrary", "arbitrary"), vmem_limit_bytes=VMEM_LIMIT),
        name="adaln_mod",
    )(c_all, ada_w, ada_b.reshape(depth, 1, n))


def _prenorm_kernel(x_ref, shift_ref, scale_ref, g_ref, o_ref, rinv_ref):
    n_chunks = x_ref.shape[0] // PRENORM_ROWS

    def rows(r):
        return pl.ds(pl.multiple_of(r * PRENORM_ROWS, PRENORM_ROWS), PRENORM_ROWS)

    def stats(r, carry):
        x = x_ref[rows(r), :]
        rinv_ref[rows(r), :] = lax.rsqrt(jnp.mean(x * x, axis=-1, keepdims=True) + NORM_EPS)
        return carry
    lax.fori_loop(0, n_chunks, stats, 0, unroll=2)

    gain = g_ref[...] * (1.0 + scale_ref[0])
    shift = shift_ref[0]

    def apply(r, carry):
        o_ref[rows(r), :] = (x_ref[rows(r), :] * rinv_ref[rows(r), :] * gain + shift).astype(o_ref.dtype)
        return carry
    lax.fori_loop(0, n_chunks, apply, 0, unroll=2)


def _mod_index(tm, rows_per_mod, mod_row):
    if mod_row is None:
        return lambda i: ((i * tm) // rows_per_mod, 0, 0)
    return lambda i: (mod_row, 0, 0)


def _prenorm(x2, shift, scale, g, *, rows_per_mod, mod_row, tm):
    m, d = x2.shape
    mod_idx = _mod_index(tm, rows_per_mod, mod_row)
    return pl.pallas_call(
        _prenorm_kernel,
        grid=(m // tm,),
        in_specs=[
            pl.BlockSpec((tm, d), lambda i: (i, 0)),
            pl.BlockSpec((1, 1, d), mod_idx),
            pl.BlockSpec((1, 1, d), mod_idx),
            pl.BlockSpec((1, d), lambda i: (0, 0)),
        ],
        out_specs=pl.BlockSpec((tm, d), lambda i: (i, 0)),
        out_shape=jax.ShapeDtypeStruct((m, d), BF16),
        scratch_shapes=[pltpu.VMEM((tm, 1), F32)],
        compiler_params=pltpu.CompilerParams(
            dimension_semantics=("arbitrary",), vmem_limit_bytes=VMEM_LIMIT),
        name="prenorm",
    )(x2, shift, scale, g)


def _inproj_kernel(*refs, rope_tiles, n_side):
    n_in = 3 + (0 if rope_tiles is None else 2)
    side_in = refs[n_in:n_in + n_side]
    side_out = refs[n_in + n_side + 1:n_in + 2 * n_side + 1]
    for src, dst in zip(side_in, side_out):
        dst[...] = src[...].astype(dst.dtype)
    refs = refs[:n_in] + (refs[n_in + n_side],) + refs[n_in + 2 * n_side + 1:]

    if rope_tiles is None:
        h_ref, w_ref, cs_ref, o_ref, wb_ref = refs
    else:
        h_ref, w_ref, cs_ref, cos_ref, sin_ref, o_ref, wb_ref = refs

    @pl.when(pl.program_id(1) == 0)
    def _():
        wb_ref[...] = w_ref[...].astype(BF16)

    row_chunks = [slice(r0, r0 + INPROJ_ROWS) for r0 in range(0, h_ref.shape[0], INPROJ_ROWS)]

    def project(rows):
        return _dot(h_ref[rows, :], wb_ref[...]) * cs_ref[...]

    def plain():
        for rows in row_chunks:
            o_ref[rows, :] = project(rows).astype(o_ref.dtype)

    if rope_tiles is None:
        plain()
        return

    j = pl.program_id(0)
    is_rope = jnp.logical_and(j >= rope_tiles[0], j < rope_tiles[1])
    pl.when(jnp.logical_not(is_rope))(plain)

    @pl.when(is_rope)
    def _():
        for rows in row_chunks:
            acc = project(rows)
            cos, sin = cos_ref[rows, :], sin_ref[rows, :]
            for c0 in range(0, acc.shape[1], RET_KEY_DIM):
                a = acc[:, c0:c0 + RET_KEY_DIM]
                o_ref[rows, c0:c0 + RET_KEY_DIM] = (
                    a * cos + pltpu.roll(a, RET_KEY_DIM // 2, 1) * sin).astype(o_ref.dtype)


def _inproj(h2, w_in, layer, col_scale, rope, *, col_map, n_out, tm, tn, side_weights=()):
    m, d = h2.shape
    n_i = m // tm
    in_specs = [
        pl.BlockSpec((tm, d), lambda j, i: (i, 0)),
        pl.BlockSpec((None, d, tn), lambda j, i: (layer, 0, col_map(j))),
        pl.BlockSpec((1, tn), lambda j, i: (0, col_map(j))),
    ]
    args = [h2, w_in, col_scale]
    rope_tiles = None
    if rope is not None:
        cosf, sinf = rope
        seq_tiles = cosf.shape[0] // tm
        in_specs += [pl.BlockSpec((tm, RET_KEY_DIM), lambda j, i: (i % seq_tiles, 0))] * 2
        args += [cosf, sinf]
        rope_tiles = (OFF_RET_Q // tn, OFF_RET_V // tn)
    out_specs = [pl.BlockSpec((tm, tn), lambda j, i: (i, j))]
    out_shape = [jax.ShapeDtypeStruct((m, n_out), BF16)]
    for w in side_weights:
        _, rows, cols = w.shape
        last = rows // SIDE_CAST_ROWS - 1
        assert last < (n_out // tn) * n_i, "not enough grid steps to cast this weight"
        chunk = lambda j, i, last=last: jnp.minimum(j * n_i + i, last)
        in_specs.append(pl.BlockSpec((None, SIDE_CAST_ROWS, cols),
                                     lambda j, i, chunk=chunk: (layer, chunk(j, i), 0)))
        args.append(w)
        out_specs.append(pl.BlockSpec((SIDE_CAST_ROWS, cols), lambda j, i, chunk=chunk: (chunk(j, i), 0)))
        out_shape.append(jax.ShapeDtypeStruct((rows, cols), BF16))
    return pl.pallas_call(
        functools.partial(_inproj_kernel, rope_tiles=rope_tiles, n_side=len(side_weights)),
        grid=(n_out // tn, n_i),
        in_specs=in_specs,
        out_specs=out_specs,
        out_shape=out_shape,
        scratch_shapes=[pltpu.VMEM((d, tn), BF16)],
        compiler_params=pltpu.CompilerParams(
            dimension_semantics=("arbitrary", "arbitrary"), vmem_limit_bytes=VMEM_LIMIT),
        name="inproj",
    )(*args)


def _na_window_row0(r, rows):
    return min(max(r - NA_WIN_ROWS // 2, 0), rows - NA_WIN_ROWS)


def _na_key_row0(i, rows):
    return min(max(4 * i - 4, 0), rows - 12)


def _na_kernel(*refs, rows, with_ctx_q):
    if with_ctx_q:
        (pb_ref, q_ref, k_ref, v_ref, kc_ref, vc_ref, qc_ref, o_ref, oc_ref,
         bias_ref, vx_ref, vcx_ref, s_ref, p_ref) = refs
    else:
        (pb_ref, q_ref, k_ref, v_ref, kc_ref, vc_ref, o_ref,
         bias_ref, vx_ref, vcx_ref, s_ref, p_ref) = refs
    n_blk = rows // 4

    @pl.when(pl.program_id(1) == 0)
    def _():
        lane = lax.broadcasted_iota(jnp.int32, (GRID_W, 2 * GRID_W), 1)
        neg = jnp.full((GRID_W, 2 * GRID_W), MASK_VALUE, F32)
        for ti, i in enumerate((0, 1, n_blk - 1)):
            krow0 = _na_key_row0(i, rows)
            for rl in range(4):
                r = 4 * i + rl
                r0 = _na_window_row0(r, rows)
                for j in range(6):
                    ka = krow0 + 2 * j
                    in_a = r0 <= ka < r0 + NA_WIN_ROWS
                    in_b = r0 <= ka + 1 < r0 + NA_WIN_ROWS
                    if in_a or in_b:
                        t = pb_ref[0, ka - r + NA_WIN_ROWS]
                        if not in_b:
                            t = jnp.where(lane < GRID_W, t, neg)
                        if not in_a:
                            t = jnp.where(lane >= GRID_W, t, neg)
                    else:
                        t = neg
                    bias_ref[ti, rl * GRID_W:(rl + 1) * GRID_W, j * 128:(j + 1) * 128] = t

    n_batch = q_ref.shape[0]
    for bi in range(n_batch):
        vx_ref[bi, :, 0:NA_DIM] = v_ref[bi]
        vx_ref[bi, :, NA_DIM:] = jnp.ones((vx_ref.shape[1], NA_DIM), BF16)
        vcx_ref[bi, :, 0:NA_DIM] = vc_ref[bi]
        vcx_ref[bi, :, NA_DIM:] = jnp.ones((vcx_ref.shape[1], NA_DIM), BF16)
    n_slots = s_ref.shape[0]
    lc = kc_ref.shape[1]

    def key_rows(i):
        k0 = _na_key_row0(i, rows) * GRID_W
        return slice(k0, k0 + NA_KBLK)

    def scores(n):
        bi, i = divmod(n, n_blk)
        q = q_ref[bi, i * NA_QBLK:(i + 1) * NA_QBLK, :]
        ti = 0 if i == 0 else (2 if i == n_blk - 1 else 1)
        s_ref[n % n_slots, :, 0:NA_KBLK] = _dot_nt(q, k_ref[bi, key_rows(i), :]) + bias_ref[ti]
        s_ref[n % n_slots, :, NA_KBLK:] = _dot_nt(q, kc_ref[bi])

    def probs(n):
        for r0 in range(0, NA_QBLK, NA_PROB_ROWS):
            s = s_ref[n % n_slots, r0:r0 + NA_PROB_ROWS, :]
            p_ref[n % n_slots, r0:r0 + NA_PROB_ROWS, :] = jnp.exp2(
                s - jnp.max(s, axis=-1, keepdims=True)).astype(BF16)

    def values(n):
        bi, i = divmod(n, n_blk)
        k0 = key_rows(i).start
        ksplit = (NA_KBLK + lc) // 2
        p = p_ref.at[n % n_slots]
        ox = (_dot(p[:, 0:ksplit], vx_ref[bi, k0:k0 + ksplit, :])
              + (_dot(p[:, ksplit:NA_KBLK], vx_ref[bi, k0 + ksplit:k0 + NA_KBLK, :])
                 + _dot(p[:, NA_KBLK:], vcx_ref[bi])))
        o_ref[bi, i * NA_QBLK:(i + 1) * NA_QBLK, :] = (
            ox[:, 0:NA_DIM] / ox[:, NA_DIM:]).astype(o_ref.dtype)

    n_total = n_batch * n_blk
    for t in range(n_total + 2):
        if 0 <= t - 2:
            values(t - 2)
        if 0 <= t - 1 < n_total:
            probs(t - 1)
        if t < n_total:
            scores(t)

    if with_ctx_q:
        for bi in range(n_batch):
            sc = _dot_nt(qc_ref[bi], kc_ref[bi])
            pc = jnp.exp2(sc - jnp.max(sc, axis=-1, keepdims=True)).astype(BF16)
            ox = _dot(pc, vcx_ref[bi])
            oc_ref[bi] = (ox[:, 0:NA_DIM] / ox[:, NA_DIM:]).astype(oc_ref.dtype)


def _na_attention(pb, u3, uc3, *, ctx_k_off, ctx_v_off, ctx_q_off):
    b, s, _ = u3.shape
    lc = uc3.shape[1]
    rows = s // GRID_W
    with_ctx_q = ctx_q_off is not None
    hb = lambda off: off // NA_DIM
    nb = BATCH_PER_STEP
    lat = lambda off: pl.BlockSpec((nb, s, NA_DIM), lambda h, bb: (bb, 0, hb(off) + h))
    cx = lambda off: pl.BlockSpec((nb, lc, NA_DIM), lambda h, bb: (bb, 0, hb(off) + h))
    in_specs = [
        pl.BlockSpec((1,) + pb.shape[1:], lambda h, bb: (h, 0, 0, 0)),
        lat(OFF_NA_Q), lat(OFF_NA_K), lat(OFF_NA_V), cx(ctx_k_off), cx(ctx_v_off),
    ]
    args = [pb, u3, u3, u3, uc3, uc3]
    out_specs = [pl.BlockSpec((nb, s, NA_DIM), lambda h, bb: (bb, 0, h))]
    out_shape = [jax.ShapeDtypeStruct((b, s, N_HEADS * NA_DIM), BF16)]
    if with_ctx_q:
        in_specs.append(cx(ctx_q_off))
        args.append(uc3)
        out_specs.append(pl.BlockSpec((nb, lc, NA_DIM), lambda h, bb: (bb, 0, h)))
        out_shape.append(jax.ShapeDtypeStruct((b, lc, N_HEADS * NA_DIM), BF16))
    outs = pl.pallas_call(
        functools.partial(_na_kernel, rows=rows, with_ctx_q=with_ctx_q),
        grid=(N_HEADS, b // nb),
        in_specs=in_specs,
        out_specs=out_specs,
        out_shape=out_shape,
        scratch_shapes=[
            pltpu.VMEM((3, NA_QBLK, NA_KBLK), F32),
            pltpu.VMEM((nb, s, 2 * NA_DIM), BF16),
            pltpu.VMEM((nb, lc, 2 * NA_DIM), BF16),
            pltpu.VMEM((NA_SLOTS, NA_QBLK, NA_KBLK + lc), F32),
            pltpu.VMEM((NA_SLOTS, NA_QBLK, NA_KBLK + lc), BF16),
        ],
        compiler_params=pltpu.CompilerParams(
            dimension_semantics=("arbitrary", "arbitrary"), vmem_limit_bytes=VMEM_LIMIT),
        name="na_attention",
    )(*args)
    return outs if with_ctx_q else (outs[0], None)


def _na_pair_bias(rpb):
    cq = jnp.arange(GRID_W)[:, None]
    ck = jnp.arange(GRID_W)[None, :]
    c0 = jnp.clip(cq - NA_WIN_COLS // 2, 0, GRID_W - NA_WIN_COLS)
    col_in = (ck >= c0) & (ck < c0 + NA_WIN_COLS)
    dc = jnp.clip(ck - cq + (NA_WIN_COLS - 1), 0, 2 * NA_WIN_COLS - 2)
    onehot = ((dc[None] == jnp.arange(2 * NA_WIN_COLS - 1)[:, None, None]) & col_in[None]).astype(F32)
    cb = jnp.einsum('hrd,dqk->hrqk', rpb.astype(F32), onehot, precision=lax.Precision.HIGHEST)
    cb = jnp.where(col_in[None, None], cb * LOG2E, MASK_VALUE)
    neg = jnp.full_like(cb[:, :1], MASK_VALUE)
    cbx = jnp.concatenate([neg, cb, neg], axis=1)
    return jnp.concatenate([cbx[:, :-1], cbx[:, 1:]], axis=-1)


def _ret_kernel(*refs, with_ctx_q):
    if with_ctx_q:
        (dl_ref, q_ref, k_ref, v_ref, kc_ref, vc_ref, qc_ref,
         o_ref, oc_ref, kv_ref, st_ref) = refs
    else:
        dl_ref, q_ref, k_ref, v_ref, kc_ref, vc_ref, o_ref, kv_ref, st_ref = refs
        qc_ref = oc_ref = None
    cs = RET_CHUNK
    dk = RET_KEY_DIM

    dl = dl_ref[...]
    lg = -(jnp.maximum(-dl, 0.0) + jnp.log1p(jnp.exp(-jnp.abs(dl))))
    head = lax.broadcasted_iota(jnp.int32, dl.shape, 1)
    lgh = jnp.sum(jnp.where(head == pl.program_id(1), lg, 0.0), axis=1, keepdims=True)
    lgf = lgh[0:1, :]
    lgb = lgh[1:2, :]

    pos = lax.broadcasted_iota(jnp.int32, (cs, dk), 0).astype(F32)
    q_dec_f = jnp.exp((pos + 1.0) * lgf)
    q_dec_b = jnp.exp((cs - pos) * lgb)
    k_dec_f = jnp.exp((cs - 1.0 - pos) * lgf)
    k_dec_b = jnp.exp(pos * lgb)
    dec_f = jnp.exp(cs * lgf)
    dec_b = jnp.exp(cs * lgb)
    ii = lax.broadcasted_iota(jnp.int32, (cs, cs), 0)
    jj = lax.broadcasted_iota(jnp.int32, (cs, cs), 1)
    diff = (ii - jj).astype(F32)
    intra = jnp.where(diff > 0, jnp.exp(jnp.maximum(diff, 0.0) * lgf),
                      jnp.where(diff < 0, jnp.exp(jnp.maximum(-diff, 0.0) * lgb), 2.0))

    def run(bi, qr, kr, vr, outr, n_tok, s0f, s0b):
        nc = n_tok // cs
        chunk = lambda t: slice(t * cs, (t + 1) * cs)

        for t in range(nc):
            sl = chunk(t)
            k = kr[bi, sl, :].astype(F32)
            kk = jnp.concatenate([k * k_dec_f, k * k_dec_b], axis=1).astype(BF16)
            kv_ref[bi, t] = _dot_tn(kk, vr[bi, sl, :])
            if outr is not None:
                st_ref[bi, t, 0:cs, :] = vr[bi, sl, :]

        sf = s0f
        for t in range(nc):
            st_ref[bi, t, cs:cs + dk, :] = sf.astype(BF16)
            sf = sf * dec_f + kv_ref[bi, t, 0:dk, :]
        sb = s0b
        for t in reversed(range(nc)):
            st_ref[bi, t, cs + dk:cs + 2 * dk, :] = sb.astype(BF16)
            sb = sb * dec_b + kv_ref[bi, t, dk:2 * dk, :]

        if outr is not None:
            for t in range(nc):
                sl = chunk(t)
                qb = qr[bi, sl, :]
                q = qb.astype(F32)
                a = _dot_nt(qb, kr[bi, sl, :])
                lhs = jnp.concatenate([a * intra, q * q_dec_f, q * q_dec_b], axis=1).astype(BF16)
                o = _dot(lhs[:, 0:cs], st_ref[bi, t, 0:cs, :]) + _dot(lhs[:, cs:], st_ref[bi, t, cs:, :])
                o = o * lax.rsqrt(jnp.mean(o * o, axis=-1, keepdims=True) + NORM_EPS)
                outr[bi, sl, :] = o.astype(outr.dtype)
        return sf, sb

    zero = jnp.zeros((dk, RET_VAL_DIM), F32)
    for bi in range(q_ref.shape[0]):
        sf, sb = run(bi, qc_ref, kc_ref, vc_ref, oc_ref, kc_ref.shape[1], zero, zero)
        run(bi, q_ref, k_ref, v_ref, o_ref, k_ref.shape[1], sf, sb)


def _retention(decay_logit, u3, uc3, *, ctx_k_off, ctx_v_off, ctx_q_off):
    b, s, _ = u3.shape
    lc = uc3.shape[1]
    with_ctx_q = ctx_q_off is not None
    kb = lambda off: off // RET_KEY_DIM
    vb = lambda off: off // RET_VAL_DIM
    nb = BATCH_PER_STEP
    in_specs = [
        pl.BlockSpec(decay_logit.shape, lambda bb, h: (0, 0)),
        pl.BlockSpec((nb, s, RET_KEY_DIM), lambda bb, h: (bb, 0, kb(OFF_RET_Q) + h)),
        pl.BlockSpec((nb, s, RET_KEY_DIM), lambda bb, h: (bb, 0, kb(OFF_RET_K) + h)),
        pl.BlockSpec((nb, s, RET_VAL_DIM), lambda bb, h: (bb, 0, vb(OFF_RET_V) + h)),
        pl.BlockSpec((nb, lc, RET_KEY_DIM), lambda bb, h: (bb, 0, kb(ctx_k_off) + h)),
        pl.BlockSpec((nb, lc, RET_VAL_DIM), lambda bb, h: (bb, 0, vb(ctx_v_off) + h)),
    ]
    args = [decay_logit, u3, u3, u3, uc3, uc3]
    out_specs = [pl.BlockSpec((nb, s, RET_VAL_DIM), lambda bb, h: (bb, 0, h))]
    out_shape = [jax.ShapeDtypeStruct((b, s, N_HEADS * RET_VAL_DIM), BF16)]
    if with_ctx_q:
        in_specs.append(pl.BlockSpec((nb, lc, RET_KEY_DIM), lambda bb, h: (bb, 0, kb(ctx_q_off) + h)))
        args.append(uc3)
        out_specs.append(pl.BlockSpec((nb, lc, RET_VAL_DIM), lambda bb, h: (bb, 0, h)))
        out_shape.append(jax.ShapeDtypeStruct((b, lc, N_HEADS * RET_VAL_DIM), BF16))
    nc = s // RET_CHUNK
    outs = pl.pallas_call(
        functools.partial(_ret_kernel, with_ctx_q=with_ctx_q),
        grid=(b // nb, N_HEADS),
        in_specs=in_specs,
        out_specs=out_specs,
        out_shape=out_shape,
        scratch_shapes=[
            pltpu.VMEM((nb, nc, 2 * RET_KEY_DIM, RET_VAL_DIM), F32),
            pltpu.VMEM((nb, nc, RET_CHUNK + 2 * RET_KEY_DIM, RET_VAL_DIM), BF16),
        ],
        compiler_params=pltpu.CompilerParams(
            dimension_semantics=("arbitrary", "arbitrary"), vmem_limit_bytes=VMEM_LIMIT),
        name="retention",
    )(*args)
    return outs if with_ctx_q else (outs[0], None)


def _merge_kernel(*refs, mode):
    (ona_ref, zna_ref, oret_ref, zret_ref, gna_ref, gret_ref, x_ref, gate_ref,
     wna_ref, wret_ref, wout_ref) = refs[:11]
    a = (ona_ref[...].astype(F32) * _silu(zna_ref[...].astype(F32))).astype(BF16)
    y_na = _dot(a, wna_ref[...])
    r = (oret_ref[...].astype(F32) * _silu(zret_ref[...].astype(F32))).astype(BF16)
    y_ret = _dot(r, wret_ref[...])
    merged = (_sigmoid(gna_ref[...].astype(F32)) * y_na
              + _sigmoid(gret_ref[...].astype(F32)) * y_ret).astype(BF16)
    if mode == "final":
        fg_ref, xn_ref = refs[11:]
    elif mode == "stream+next":
        g_ref, scale_ref, shift_ref, xn_ref, h_ref = refs[11:]
    else:
        g_ref, scale_ref, shift_ref, h_ref, xn_ref = refs[11:]

    d = x_ref.shape[1]
    ssq = jnp.zeros((x_ref.shape[0], 128), F32)
    for c0 in range(0, d, MERGE_TN):
        cols = slice(c0, c0 + MERGE_TN)
        xn = x_ref[:, cols] + gate_ref[0, :, cols] * _dot(merged, wout_ref[:, cols])
        xn_ref[:, cols] = xn
        for l0 in range(0, MERGE_TN, 128):
            ssq = ssq + xn[:, l0:l0 + 128] * xn[:, l0:l0 + 128]
    rinv = lax.rsqrt(jnp.sum(ssq, axis=-1, keepdims=True) * (1.0 / d) + NORM_EPS)
    if mode == "final":
        xn_ref[...] = xn_ref[...] * rinv * fg_ref[...]
    else:
        gain = g_ref[...] * (1.0 + scale_ref[0])
        h_ref[...] = (xn_ref[...] * rinv * gain + shift_ref[0]).astype(h_ref.dtype)


def _merge(o_na, o_ret, u2, x2, gate, wna, wret, wout, *, mode, extra, rows_per_mod, mod_row, tm):
    m, d = x2.shape
    mod_idx = _mod_index(tm, rows_per_mod, mod_row)
    resident = lambda shape: pl.BlockSpec(shape, lambda i: (0, 0), pipeline_mode=pl.Buffered(1))
    row_tile = lambda width, col_blk: pl.BlockSpec((tm, width), lambda i: (i, col_blk))
    w_na_cols = N_HEADS * NA_DIM
    w_ret_cols = N_HEADS * RET_VAL_DIM
    in_specs = [
        row_tile(w_na_cols, 0),
        row_tile(w_na_cols, OFF_NA_Z // w_na_cols),
        row_tile(w_ret_cols, 0),
        row_tile(w_ret_cols, OFF_RET_Z // w_ret_cols),
        row_tile(d, OFF_G_NA // d),
        row_tile(d, OFF_G_RET // d),
        row_tile(d, 0),
        pl.BlockSpec((1, 1, d), mod_idx),
        resident(wna.shape),
        resident(wret.shape),
        resident(wout.shape),
    ]
    args = [o_na, u2, o_ret, u2, u2, u2, x2, gate, wna, wret, wout]
    stream = (row_tile(d, 0), jax.ShapeDtypeStruct((m, d), F32))
    nxt = (row_tile(d, 0), jax.ShapeDtypeStruct((m, d), BF16))
    if mode == "final":
        in_specs.append(pl.BlockSpec((1, d), lambda i: (0, 0)))
        args.append(extra)
        outs = [stream]
    else:
        g_next, scale_next, shift_next = extra
        in_specs += [pl.BlockSpec((1, d), lambda i: (0, 0)),
                     pl.BlockSpec((1, 1, d), mod_idx), pl.BlockSpec((1, 1, d), mod_idx)]
        args += [g_next, scale_next, shift_next]
        outs = [stream, nxt] if mode == "stream+next" else [nxt]
    return pl.pallas_call(
        functools.partial(_merge_kernel, mode=mode),
        grid=(m // tm,),
        in_specs=in_specs,
        out_specs=[o[0] for o in outs],
        out_shape=[o[1] for o in outs],
        scratch_shapes=[pltpu.VMEM((tm, d), F32)] if mode == "next" else [],
        compiler_params=pltpu.CompilerParams(
            dimension_semantics=("arbitrary",), vmem_limit_bytes=VMEM_LIMIT),
        name="merge",
    )(*args)


def _rope_tables(n_tokens):
    t = jnp.arange(n_tokens)
    row = (t // GRID_W).astype(F32)
    col = (t % GRID_W).astype(F32)
    n_freq = RET_KEY_DIM // 4
    inv_freq = ROPE_BASE ** (-jnp.arange(n_freq, dtype=F32) / n_freq)
    ang = jnp.concatenate([row[:, None] * inv_freq, col[:, None] * inv_freq], axis=-1)
    cos, sin = jnp.cos(ang), jnp.sin(ang)
    return jnp.concatenate([cos, cos], axis=-1), jnp.concatenate([-sin, sin], axis=-1)


def kernel(x, c, ctx, c_ctx, ada_w, ada_b, norm_g, w_in, na_rpb, ret_decay_logit,
           w_proj_na, w_proj_ret, w_out, final_g):
    b, s, d = x.shape
    lc = ctx.shape[1]
    depth = ada_w.shape[0]

    c_all = jnp.concatenate([c, c_ctx[None], jnp.zeros((MOD_ROWS - b - 1, d), F32)], axis=0)
    mod = _modulation(c_all, ada_w, ada_b)
    rope = _rope_tables(s)
    cols = jnp.arange(IN_COLS)
    col_scale = jnp.where((cols >= OFF_NA_Q) & (cols < OFF_NA_K), NA_DIM ** -0.5 * LOG2E,
                          jnp.where((cols >= OFF_RET_K) & (cols < OFF_RET_V), RET_KEY_DIM ** -0.5, 1.0)
                          ).astype(F32).reshape(1, IN_COLS)

    def modulation(l):
        shift = mod[l, :, 0:d].reshape(MOD_ROWS, 1, d)
        scale = mod[l, :, d:2 * d].reshape(MOD_ROWS, 1, d)
        gate = mod[l, :, 2 * d:3 * d].reshape(MOD_ROWS, 1, d)
        return norm_g[l].reshape(1, d), scale, shift, gate

    x_lat = x.reshape(b * s, d)
    x_ctx = ctx.reshape(b * lc, d)
    g, scale, shift, gate = modulation(0)
    h_lat = _prenorm(x_lat, shift, scale, g, rows_per_mod=s, mod_row=None, tm=PRENORM_TM)
    h_ctx = _prenorm(x_ctx, shift, scale, g, rows_per_mod=lc, mod_row=CTX_MOD_ROW, tm=PRENORM_TM)
    for l in range(depth):
        last = l == depth - 1
        u, wna, wret, wout = _inproj(h_lat, w_in, l, col_scale, rope, col_map=lambda j: j,
                                     n_out=IN_COLS, tm=INPROJ_TM, tn=INPROJ_TN,
                                     side_weights=(w_proj_na, w_proj_ret, w_out))
        if last:
            (uc,) = _inproj(h_ctx, w_in, l, col_scale, None,
                            col_map=lambda j: jnp.where(j < 2, j + 1, j + 3),
                            n_out=CTX_KV_COLS, tm=INPROJ_TM, tn=INPROJ_TN)
            offs = dict(na_k=COFF_NA_K, na_v=COFF_NA_V, ret_k=COFF_RET_K, ret_v=COFF_RET_V)
            na_q = ret_q = None
        else:
            (uc,) = _inproj(h_ctx, w_in, l, col_scale, None, col_map=lambda j: j, n_out=IN_COLS,
                            tm=INPROJ_TM, tn=INPROJ_TN)
            offs = dict(na_k=OFF_NA_K, na_v=OFF_NA_V, ret_k=OFF_RET_K, ret_v=OFF_RET_V)
            na_q, ret_q = OFF_NA_Q, OFF_RET_Q
        u3 = u.reshape(b, s, IN_COLS)
        uc3 = uc.reshape(b, lc, uc.shape[-1])

        o_na, o_na_ctx = _na_attention(_na_pair_bias(na_rpb[l]), u3, uc3,
                                       ctx_k_off=offs['na_k'], ctx_v_off=offs['na_v'], ctx_q_off=na_q)
        o_ret, o_ret_ctx = _retention(ret_decay_logit[l], u3, uc3,
                                      ctx_k_off=offs['ret_k'], ctx_v_off=offs['ret_v'], ctx_q_off=ret_q)

        lat_args = (o_na.reshape(b * s, -1), o_ret.reshape(b * s, -1), u, x_lat, gate, wna, wret, wout)
        if last:
            (x_lat,) = _merge(*lat_args, mode="final", extra=final_g.reshape(1, d),
                              rows_per_mod=s, mod_row=None, tm=MERGE_TM)
            break
        g, scale, shift, gate_next = modulation(l + 1)
        x_lat, h_lat = _merge(*lat_args, mode="stream+next", extra=(g, scale, shift),
                              rows_per_mod=s, mod_row=None, tm=MERGE_TM)
        ctx_args = (o_na_ctx.reshape(b * lc, -1), o_ret_ctx.reshape(b * lc, -1), uc, x_ctx,
                    gate, wna, wret, wout)
        if l + 1 == depth - 1:
            (h_ctx,) = _merge(*ctx_args, mode="next", extra=(g, scale, shift),
                              rows_per_mod=lc, mod_row=CTX_MOD_ROW, tm=MERGE_TM)
        else:
            x_ctx, h_ctx = _merge(*ctx_args, mode="stream+next", extra=(g, scale, shift),
                                  rows_per_mod=lc, mod_row=CTX_MOD_ROW, tm=MERGE_TM)
        gate = gate_next
    return x_lat.reshape(b, s, d)
```

```python
import functools

import numpy as np
import jax
import jax.numpy as jnp
from jax import lax
from jax.experimental import pallas as pl
from jax.experimental.pallas import tpu as pltpu

F32 = jnp.float32
BF16 = jnp.bfloat16

D_MODEL = 2048
GRID_W = 64
N_HEADS = 8
NA_DIM = 128
NA_WIN_ROWS = 8
NA_WIN_COLS = 16
RET_KEY_DIM = 128
RET_VAL_DIM = 256
RET_CHUNK = 256
ROPE_BASE = 10000.0
NORM_EPS = 1e-6
MASK_VALUE = -1e30
LOG2E = 1.4426950408889634

OFF_NA_Q, OFF_NA_K, OFF_NA_V, OFF_NA_Z = 0, 1024, 2048, 3072
OFF_RET_Q, OFF_RET_K, OFF_RET_V, OFF_RET_Z = 4096, 5120, 6144, 8192
OFF_G_NA, OFF_G_RET = 10240, 12288
IN_COLS = 14336
CTX_KV_COLS = 5120
COFF_NA_K, COFF_NA_V, COFF_RET_K, COFF_RET_V = 0, 1024, 2048, 3072

MOD_ROWS = 16
CTX_MOD_ROW = 8

NA_QBLK = 4 * GRID_W
NA_KBLK = 12 * GRID_W

VMEM_LIMIT = 56 * 1024 * 1024
PRENORM_TM = 2048
PRENORM_ROWS = 64
INPROJ_TM = 2048
INPROJ_TN = 1024
INPROJ_ROWS = 512
SIDE_CAST_ROWS = 64
MERGE_TM = 256
MERGE_TN = 512
NA_PROB_ROWS = 32
NA_SLOTS = 2
BATCH_PER_STEP = 4


def _dot(a, b):
    return jnp.dot(a, b, preferred_element_type=F32)


def _dot_nt(a, b):
    return lax.dot_general(a, b, (((1,), (1,)), ((), ())), preferred_element_type=F32)


def _dot_tn(a, b):
    return lax.dot_general(a, b, (((0,), (0,)), ((), ())), preferred_element_type=F32)


def _sigmoid(x):
    return 0.5 * jnp.tanh(0.5 * x) + 0.5


def _silu(x):
    return x * _sigmoid(x)


def _mod_kernel(c_ref, w_ref, b_ref, o_ref):
    c = c_ref[...]
    cs = (c * _sigmoid(c)).astype(BF16)
    o_ref[0] = _dot(cs, w_ref[0].astype(BF16)) + b_ref[0]


def _modulation(c_all, ada_w, ada_b, tn=2048):
    depth, d, n = ada_w.shape
    return pl.pallas_call(
        _mod_kernel,
        grid=(depth, n // tn),
        in_specs=[
            pl.BlockSpec((MOD_ROWS, d), lambda l, j: (0, 0)),
            pl.BlockSpec((1, d, tn), lambda l, j: (l, 0, j)),
            pl.BlockSpec((1, 1, tn), lambda l, j: (l, 0, j)),
        ],
        out_specs=pl.BlockSpec((1, MOD_ROWS, tn), lambda l, j: (l, 0, j)),
        out_shape=jax.ShapeDtypeStruct((depth, MOD_ROWS, n), F32),
        compiler_params=pltpu.CompilerParams(
            dimension_semantics=("arbitrary", "arbitrary"), vmem_limit_bytes=VMEM_LIMIT),
        name="adaln_mod",
    )(c_all, ada_w, ada_b.reshape(depth, 1, n))


def _prenorm_kernel(x_ref, shift_ref, scale_ref, g_ref, o_ref, rinv_ref):
    n_chunks = x_ref.shape[0] // PRENORM_ROWS

    def rows(r):
        return pl.ds(pl.multiple_of(r * PRENORM_ROWS, PRENORM_ROWS), PRENORM_ROWS)

    def stats(r, carry):
        x = x_ref[rows(r), :]
        rinv_ref[rows(r), :] = lax.rsqrt(jnp.mean(x * x, axis=-1, keepdims=True) + NORM_EPS)
        return carry
    lax.fori_loop(0, n_chunks, stats, 0, unroll=2)

    gain = g_ref[...] * (1.0 + scale_ref[0])
    shift = shift_ref[0]

    def apply(r, carry):
        o_ref[rows(r), :] = (x_ref[rows(r), :] * rinv_ref[rows(r), :] * gain + shift).astype(o_ref.dtype)
        return carry
    lax.fori_loop(0, n_chunks, apply, 0, unroll=2)


def _mod_index(tm, rows_per_mod, mod_row):
    if mod_row is None:
        return lambda i: ((i * tm) // rows_per_mod, 0, 0)
    return lambda i: (mod_row, 0, 0)


def _prenorm(x2, shift, scale, g, *, rows_per_mod, mod_row, tm):
    m, d = x2.shape
    mod_idx = _mod_index(tm, rows_per_mod, mod_row)
    return pl.pallas_call(
        _prenorm_kernel,
        grid=(m // tm,),
        in_specs=[
            pl.BlockSpec((tm, d), lambda i: (i, 0)),
            pl.BlockSpec((1, 1, d), mod_idx),
            pl.BlockSpec((1, 1, d), mod_idx),
            pl.BlockSpec((1, d), lambda i: (0, 0)),
        ],
        out_specs=pl.BlockSpec((tm, d), lambda i: (i, 0)),
        out_shape=jax.ShapeDtypeStruct((m, d), BF16),
        scratch_shapes=[pltpu.VMEM((tm, 1), F32)],
        compiler_params=pltpu.CompilerParams(
            dimension_semantics=("arbitrary",), vmem_limit_bytes=VMEM_LIMIT),
        name="prenorm",
    )(x2, shift, scale, g)


def _inproj_kernel(*refs, rope_tiles, n_side):
    n_in = 3 + (0 if rope_tiles is None else 2)
    side_in = refs[n_in:n_in + n_side]
    side_out = refs[n_in + n_side + 1:n_in + 2 * n_side + 1]
    for src, dst in zip(side_in, side_out):
        dst[...] = src[...].astype(dst.dtype)
    refs = refs[:n_in] + (refs[n_in + n_side],) + refs[n_in + 2 * n_side + 1:]

    if rope_tiles is None:
        h_ref, w_ref, cs_ref, o_ref, wb_ref = refs
    else:
        h_ref, w_ref, cs_ref, cos_ref, sin_ref, o_ref, wb_ref = refs

    @pl.when(pl.program_id(1) == 0)
    def _():
        wb_ref[...] = w_ref[...].astype(BF16)

    row_chunks = [slice(r0, r0 + INPROJ_ROWS) for r0 in range(0, h_ref.shape[0], INPROJ_ROWS)]

    def project(rows):
        return _dot(h_ref[rows, :], wb_ref[...]) * cs_ref[...]

    def plain():
        for rows in row_chunks:
            o_ref[rows, :] = project(rows).astype(o_ref.dtype)

    if rope_tiles is None:
        plain()
        return

    j = pl.program_id(0)
    is_rope = jnp.logical_and(j >= rope_tiles[0], j < rope_tiles[1])
    pl.when(jnp.logical_not(is_rope))(plain)

    @pl.when(is_rope)
    def _():
        for rows in row_chunks:
            acc = project(rows)
            cos, sin = cos_ref[rows, :], sin_ref[rows, :]
            for c0 in range(0, acc.shape[1], RET_KEY_DIM):
                a = acc[:, c0:c0 + RET_KEY_DIM]
                o_ref[rows, c0:c0 + RET_KEY_DIM] = (
                    a * cos + pltpu.roll(a, RET_KEY_DIM // 2, 1) * sin).astype(o_ref.dtype)


def _inproj(h2, w_in, layer, col_scale, rope, *, col_map, n_out, tm, tn, side_weights=()):
    m, d = h2.shape
    n_i = m // tm
    in_specs = [
        pl.BlockSpec((tm, d), lambda j, i: (i, 0)),
        pl.BlockSpec((None, d, tn), lambda j, i: (layer, 0, col_map(j))),
        pl.BlockSpec((1, tn), lambda j, i: (0, col_map(j))),
    ]
    args = [h2, w_in, col_scale]
    rope_tiles = None
    if rope is not None:
        cosf, sinf = rope
        seq_tiles = cosf.shape[0] // tm
        in_specs += [pl.BlockSpec((tm, RET_KEY_DIM), lambda j, i: (i % seq_tiles, 0))] * 2
        args += [cosf, sinf]
        rope_tiles = (OFF_RET_Q // tn, OFF_RET_V // tn)
    out_specs = [pl.BlockSpec((tm, tn), lambda j, i: (i, j))]
    out_shape = [jax.ShapeDtypeStruct((m, n_out), BF16)]
    for w in side_weights:
        _, rows, cols = w.shape
        last = rows // SIDE_CAST_ROWS - 1
        assert last < (n_out // tn) * n_i, "not enough grid steps to cast this weight"
        chunk = lambda j, i, last=last: jnp.minimum(j * n_i + i, last)
        in_specs.append(pl.BlockSpec((None, SIDE_CAST_ROWS, cols),
                                     lambda j, i, chunk=chunk: (layer, chunk(j, i), 0)))
        args.append(w)
        out_specs.append(pl.BlockSpec((SIDE_CAST_ROWS, cols), lambda j, i, chunk=chunk: (chunk(j, i), 0)))
        out_shape.append(jax.ShapeDtypeStruct((rows, cols), BF16))
    return pl.pallas_call(
        functools.partial(_inproj_kernel, rope_tiles=rope_tiles, n_side=len(side_weights)),
        grid=(n_out // tn, n_i),
        in_specs=in_specs,
        out_specs=out_specs,
        out_shape=out_shape,
        scratch_shapes=[pltpu.VMEM((d, tn), BF16)],
        compiler_params=pltpu.CompilerParams(
            dimension_semantics=("arbitrary", "arbitrary"), vmem_limit_bytes=VMEM_LIMIT),
        name="inproj",
    )(*args)


def _na_window_row0(r, rows):
    return min(max(r - NA_WIN_ROWS // 2, 0), rows - NA_WIN_ROWS)


def _na_key_row0(i, rows):
    return min(max(4 * i - 4, 0), rows - 12)


def _na_kernel(*refs, rows, with_ctx_q):
    if with_ctx_q:
        (pb_ref, q_ref, k_ref, v_ref, kc_ref, vc_ref, qc_ref, o_ref, oc_ref,
         bias_ref, vx_ref, vcx_ref, s_ref, p_ref) = refs
    else:
        (pb_ref, q_ref, k_ref, v_ref, kc_ref, vc_ref, o_ref,
         bias_ref, vx_ref, vcx_ref, s_ref, p_ref) = refs
    n_blk = rows // 4

    @pl.when(pl.program_id(1) == 0)
    def _():
        lane = lax.broadcasted_iota(jnp.int32, (GRID_W, 2 * GRID_W), 1)
        neg = jnp.full((GRID_W, 2 * GRID_W), MASK_VALUE, F32)
        for ti, i in enumerate((0, 1, n_blk - 1)):
            krow0 = _na_key_row0(i, rows)
            for rl in range(4):
                r = 4 * i + rl
                r0 = _na_window_row0(r, rows)
                for j in range(6):
                    ka = krow0 + 2 * j
                    in_a = r0 <= ka < r0 + NA_WIN_ROWS
                    in_b = r0 <= ka + 1 < r0 + NA_WIN_ROWS
                    if in_a or in_b:
                        t = pb_ref[0, ka - r + NA_WIN_ROWS]
                        if not in_b:
                            t = jnp.where(lane < GRID_W, t, neg)
                        if not in_a:
                            t = jnp.where(lane >= GRID_W, t, neg)
                    else:
                        t = neg
                    bias_ref[ti, rl * GRID_W:(rl + 1) * GRID_W, j * 128:(j + 1) * 128] = t

    n_batch = q_ref.shape[0]
    for bi in range(n_batch):
        vx_ref[bi, :, 0:NA_DIM] = v_ref[bi]
        vx_ref[bi, :, NA_DIM:] = jnp.ones((vx_ref.shape[1], NA_DIM), BF16)
        vcx_ref[bi, :, 0:NA_DIM] = vc_ref[bi]
        vcx_ref[bi, :, NA_DIM:] = jnp.ones((vcx_ref.shape[1], NA_DIM), BF16)
    n_slots = s_ref.shape[0]
    lc = kc_ref.shape[1]

    def key_rows(i):
        k0 = _na_key_row0(i, rows) * GRID_W
        return slice(k0, k0 + NA_KBLK)

    def scores(n):
        bi, i = divmod(n, n_blk)
        q = q_ref[bi, i * NA_QBLK:(i + 1) * NA_QBLK, :]
        ti = 0 if i == 0 else (2 if i == n_blk - 1 else 1)
        s_ref[n % n_slots, :, 0:NA_KBLK] = _dot_nt(q, k_ref[bi, key_rows(i), :]) + bias_ref[ti]
        s_ref[n % n_slots, :, NA_KBLK:] = _dot_nt(q, kc_ref[bi])

    def probs(n):
        for r0 in range(0, NA_QBLK, NA_PROB_ROWS):
            s = s_ref[n % n_slots, r0:r0 + NA_PROB_ROWS, :]
            p_ref[n % n_slots, r0:r0 + NA_PROB_ROWS, :] = jnp.exp2(
                s - jnp.max(s, axis=-1, keepdims=True)).astype(BF16)

    def values(n):
        bi, i = divmod(n, n_blk)
        k0 = key_rows(i).start
        ksplit = (NA_KBLK + lc) // 2
        p = p_ref.at[n % n_slots]
        ox = (_dot(p[:, 0:ksplit], vx_ref[bi, k0:k0 + ksplit, :])
              + (_dot(p[:, ksplit:NA_KBLK], vx_ref[bi, k0 + ksplit:k0 + NA_KBLK, :])
                 + _dot(p[:, NA_KBLK:], vcx_ref[bi])))
        o_ref[bi, i * NA_QBLK:(i + 1) * NA_QBLK, :] = (
            ox[:, 0:NA_DIM] / ox[:, NA_DIM:]).astype(o_ref.dtype)

    n_total = n_batch * n_blk
    for t in range(n_total + 2):
        if 0 <= t - 2:
            values(t - 2)
        if 0 <= t - 1 < n_total:
            probs(t - 1)
        if t < n_total:
            scores(t)

    if with_ctx_q:
        for bi in range(n_batch):
            sc = _dot_nt(qc_ref[bi], kc_ref[bi])
            pc = jnp.exp2(sc - jnp.max(sc, axis=-1, keepdims=True)).astype(BF16)
            ox = _dot(pc, vcx_ref[bi])
            oc_ref[bi] = (ox[:, 0:NA_DIM] / ox[:, NA_DIM:]).astype(oc_ref.dtype)


def _na_attention(pb, u3, uc3, *, ctx_k_off, ctx_v_off, ctx_q_off):
    b, s, _ = u3.shape
    lc = uc3.shape[1]
    rows = s // GRID_W
    with_ctx_q = ctx_q_off is not None
    hb = lambda off: off // NA_DIM
    nb = BATCH_PER_STEP
    lat = lambda off: pl.BlockSpec((nb, s, NA_DIM), lambda h, bb: (bb, 0, hb(off) + h))
    cx = lambda off: pl.BlockSpec((nb, lc, NA_DIM), lambda h, bb: (bb, 0, hb(off) + h))
    in_specs = [
        pl.BlockSpec((1,) + pb.shape[1:], lambda h, bb: (h, 0, 0, 0)),
        lat(OFF_NA_Q), lat(OFF_NA_K), lat(OFF_NA_V), cx(ctx_k_off), cx(ctx_v_off),
    ]
    args = [pb, u3, u3, u3, uc3, uc3]
    out_specs = [pl.BlockSpec((nb, s, NA_DIM), lambda h, bb: (bb, 0, h))]
    out_shape = [jax.ShapeDtypeStruct((b, s, N_HEADS * NA_DIM), BF16)]
    if with_ctx_q:
        in_specs.append(cx(ctx_q_off))
        args.append(uc3)
        out_specs.append(pl.BlockSpec((nb, lc, NA_DIM), lambda h, bb: (bb, 0, h)))
        out_shape.append(jax.ShapeDtypeStruct((b, lc, N_HEADS * NA_DIM), BF16))
    outs = pl.pallas_call(
        functools.partial(_na_kernel, rows=rows, with_ctx_q=with_ctx_q),
        grid=(N_HEADS, b // nb),
        in_specs=in_specs,
        out_specs=out_specs,
        out_shape=out_shape,
        scratch_shapes=[
            pltpu.VMEM((3, NA_QBLK, NA_KBLK), F32),
            pltpu.VMEM((nb, s, 2 * NA_DIM), BF16),
            pltpu.VMEM((nb, lc, 2 * NA_DIM), BF16),
            pltpu.VMEM((NA_SLOTS, NA_QBLK, NA_KBLK + lc), F32),
            pltpu.VMEM((NA_SLOTS, NA_QBLK, NA_KBLK + lc), BF16),
        ],
        compiler_params=pltpu.CompilerParams(
            dimension_semantics=("arbitrary", "arbitrary"), vmem_limit_bytes=VMEM_LIMIT),
        name="na_attention",
    )(*args)
    return outs if with_ctx_q else (outs[0], None)


def _na_pair_bias(rpb):
    cq = np.arange(GRID_W)[:, None]
    ck = np.arange(GRID_W)[None, :]
    c0 = np.clip(cq - NA_WIN_COLS // 2, 0, GRID_W - NA_WIN_COLS)
    col_in = (ck >= c0) & (ck < c0 + NA_WIN_COLS)
    dc = np.clip(ck - cq + (NA_WIN_COLS - 1), 0, 2 * NA_WIN_COLS - 2)
    onehot = ((dc[None] == np.arange(2 * NA_WIN_COLS - 1)[:, None, None]) & col_in[None]).astype(np.float32)
    cb = jnp.einsum('hrd,dqk->hrqk', rpb.astype(F32), onehot, precision=lax.Precision.HIGHEST)
    cb = jnp.where(col_in[None, None], cb * LOG2E, MASK_VALUE)
    neg = jnp.full_like(cb[:, :1], MASK_VALUE)
    cbx = jnp.concatenate([neg, cb, neg], axis=1)
    return jnp.concatenate([cbx[:, :-1], cbx[:, 1:]], axis=-1)


def _ret_kernel(*refs, with_ctx_q):
    if with_ctx_q:
        (dl_ref, q_ref, k_ref, v_ref, kc_ref, vc_ref, qc_ref,
         o_ref, oc_ref, kv_ref, st_ref) = refs
    else:
        dl_ref, q_ref, k_ref, v_ref, kc_ref, vc_ref, o_ref, kv_ref, st_ref = refs
        qc_ref = oc_ref = None
    cs = RET_CHUNK
    dk = RET_KEY_DIM

    dl = dl_ref[...]
    lg = -(jnp.maximum(-dl, 0.0) + jnp.log1p(jnp.exp(-jnp.abs(dl))))
    head = lax.broadcasted_iota(jnp.int32, dl.shape, 1)
    lgh = jnp.sum(jnp.where(head == pl.program_id(1), lg, 0.0), axis=1, keepdims=True)
    lgf = lgh[0:1, :]
    lgb = lgh[1:2, :]

    pos = lax.broadcasted_iota(jnp.int32, (cs, dk), 0).astype(F32)
    q_dec_f = jnp.exp((pos + 1.0) * lgf)
    q_dec_b = jnp.exp((cs - pos) * lgb)
    k_dec_f = jnp.exp((cs - 1.0 - pos) * lgf)
    k_dec_b = jnp.exp(pos * lgb)
    dec_f = jnp.exp(cs * lgf)
    dec_b = jnp.exp(cs * lgb)
    ii = lax.broadcasted_iota(jnp.int32, (cs, cs), 0)
    jj = lax.broadcasted_iota(jnp.int32, (cs, cs), 1)
    diff = (ii - jj).astype(F32)
    intra = jnp.where(diff > 0, jnp.exp(jnp.maximum(diff, 0.0) * lgf),
                      jnp.where(diff < 0, jnp.exp(jnp.maximum(-diff, 0.0) * lgb), 2.0))

    def run(bi, qr, kr, vr, outr, n_tok, s0f, s0b):
        nc = n_tok // cs
        chunk = lambda t: slice(t * cs, (t + 1) * cs)

        for t in range(nc):
            sl = chunk(t)
            k = kr[bi, sl, :].astype(F32)
            kk = jnp.concatenate([k * k_dec_f, k * k_dec_b], axis=1).astype(BF16)
            kv_ref[bi, t] = _dot_tn(kk, vr[bi, sl, :])
            if outr is not None:
                st_ref[bi, t, 0:cs, :] = vr[bi, sl, :]

        sf = s0f
        for t in range(nc):
            st_ref[bi, t, cs:cs + dk, :] = sf.astype(BF16)
            sf = sf * dec_f + kv_ref[bi, t, 0:dk, :]
        sb = s0b
        for t in reversed(range(nc)):
            st_ref[bi, t, cs + dk:cs + 2 * dk, :] = sb.astype(BF16)
            sb = sb * dec_b + kv_ref[bi, t, dk:2 * dk, :]

        if outr is not None:
            for t in range(nc):
                sl = chunk(t)
                qb = qr[bi, sl, :]
                q = qb.astype(F32)
                a = _dot_nt(qb, kr[bi, sl, :])
                lhs = jnp.concatenate([a * intra, q * q_dec_f, q * q_dec_b], axis=1).astype(BF16)
                o = _dot(lhs[:, 0:cs], st_ref[bi, t, 0:cs, :]) + _dot(lhs[:, cs:], st_ref[bi, t, cs:, :])
                o = o * lax.rsqrt(jnp.mean(o * o, axis=-1, keepdims=True) + NORM_EPS)
                outr[bi, sl, :] = o.astype(outr.dtype)
        return sf, sb

    zero = jnp.zeros((dk, RET_VAL_DIM), F32)
    for bi in range(q_ref.shape[0]):
        sf, sb = run(bi, qc_ref, kc_ref, vc_ref, oc_ref, kc_ref.shape[1], zero, zero)
        run(bi, q_ref, k_ref, v_ref, o_ref, k_ref.shape[1], sf, sb)


def _retention(decay_logit, u3, uc3, *, ctx_k_off, ctx_v_off, ctx_q_off):
    b, s, _ = u3.shape
    lc = uc3.shape[1]
    with_ctx_q = ctx_q_off is not None
    kb = lambda off: off // RET_KEY_DIM
    vb = lambda off: off // RET_VAL_DIM
    nb = BATCH_PER_STEP
    in_specs = [
        pl.BlockSpec(decay_logit.shape, lambda bb, h: (0, 0)),
        pl.BlockSpec((nb, s, RET_KEY_DIM), lambda bb, h: (bb, 0, kb(OFF_RET_Q) + h)),
        pl.BlockSpec((nb, s, RET_KEY_DIM), lambda bb, h: (bb, 0, kb(OFF_RET_K) + h)),
        pl.BlockSpec((nb, s, RET_VAL_DIM), lambda bb, h: (bb, 0, vb(OFF_RET_V) + h)),
        pl.BlockSpec((nb, lc, RET_KEY_DIM), lambda bb, h: (bb, 0, kb(ctx_k_off) + h)),
        pl.BlockSpec((nb, lc, RET_VAL_DIM), lambda bb, h: (bb, 0, vb(ctx_v_off) + h)),
    ]
    args = [decay_logit, u3, u3, u3, uc3, uc3]
    out_specs = [pl.BlockSpec((nb, s, RET_VAL_DIM), lambda bb, h: (bb, 0, h))]
    out_shape = [jax.ShapeDtypeStruct((b, s, N_HEADS * RET_VAL_DIM), BF16)]
    if with_ctx_q:
        in_specs.append(pl.BlockSpec((nb, lc, RET_KEY_DIM), lambda bb, h: (bb, 0, kb(ctx_q_off) + h)))
        args.append(uc3)
        out_specs.append(pl.BlockSpec((nb, lc, RET_VAL_DIM), lambda bb, h: (bb, 0, h)))
        out_shape.append(jax.ShapeDtypeStruct((b, lc, N_HEADS * RET_VAL_DIM), BF16))
    nc = s // RET_CHUNK
    outs = pl.pallas_call(
        functools.partial(_ret_kernel, with_ctx_q=with_ctx_q),
        grid=(b // nb, N_HEADS),
        in_specs=in_specs,
        out_specs=out_specs,
        out_shape=out_shape,
        scratch_shapes=[
            pltpu.VMEM((nb, nc, 2 * RET_KEY_DIM, RET_VAL_DIM), F32),
            pltpu.VMEM((nb, nc, RET_CHUNK + 2 * RET_KEY_DIM, RET_VAL_DIM), BF16),
        ],
        compiler_params=pltpu.CompilerParams(
            dimension_semantics=("arbitrary", "arbitrary"), vmem_limit_bytes=VMEM_LIMIT),
        name="retention",
    )(*args)
    return outs if with_ctx_q else (outs[0], None)


def _merge_kernel(*refs, mode):
    (ona_ref, zna_ref, oret_ref, zret_ref, gna_ref, gret_ref, x_ref, gate_ref,
     wna_ref, wret_ref, wout_ref) = refs[:11]
    a = (ona_ref[...].astype(F32) * _silu(zna_ref[...].astype(F32))).astype(BF16)
    y_na = _dot(a, wna_ref[...])
    r = (oret_ref[...].astype(F32) * _silu(zret_ref[...].astype(F32))).astype(BF16)
    y_ret = _dot(r, wret_ref[...])
    merged = (_sigmoid(gna_ref[...].astype(F32)) * y_na
              + _sigmoid(gret_ref[...].astype(F32)) * y_ret).astype(BF16)
    if mode == "final":
        fg_ref, xn_ref = refs[11:]
    elif mode == "stream+next":
        g_ref, scale_ref, shift_ref, xn_ref, h_ref = refs[11:]
    else:
        g_ref, scale_ref, shift_ref, h_ref, xn_ref = refs[11:]

    d = x_ref.shape[1]
    ssq = jnp.zeros((x_ref.shape[0], 128), F32)
    for c0 in range(0, d, MERGE_TN):
        cols = slice(c0, c0 + MERGE_TN)
        xn = x_ref[:, cols] + gate_ref[0, :, cols] * _dot(merged, wout_ref[:, cols])
        xn_ref[:, cols] = xn
        for l0 in range(0, MERGE_TN, 128):
            ssq = ssq + xn[:, l0:l0 + 128] * xn[:, l0:l0 + 128]
    rinv = lax.rsqrt(jnp.sum(ssq, axis=-1, keepdims=True) * (1.0 / d) + NORM_EPS)
    if mode == "final":
        xn_ref[...] = xn_ref[...] * rinv * fg_ref[...]
    else:
        gain = g_ref[...] * (1.0 + scale_ref[0])
        h_ref[...] = (xn_ref[...] * rinv * gain + shift_ref[0]).astype(h_ref.dtype)


def _merge(o_na, o_ret, u2, x2, gate, wna, wret, wout, *, mode, extra, rows_per_mod, mod_row, tm):
    m, d = x2.shape
    mod_idx = _mod_index(tm, rows_per_mod, mod_row)
    resident = lambda shape: pl.BlockSpec(shape, lambda i: (0, 0), pipeline_mode=pl.Buffered(1))
    row_tile = lambda width, col_blk: pl.BlockSpec((tm, width), lambda i: (i, col_blk))
    w_na_cols = N_HEADS * NA_DIM
    w_ret_cols = N_HEADS * RET_VAL_DIM
    in_specs = [
        row_tile(w_na_cols, 0),
        row_tile(w_na_cols, OFF_NA_Z // w_na_cols),
        row_tile(w_ret_cols, 0),
        row_tile(w_ret_cols, OFF_RET_Z // w_ret_cols),
        row_tile(d, OFF_G_NA // d),
        row_tile(d, OFF_G_RET // d),
        row_tile(d, 0),
        pl.BlockSpec((1, 1, d), mod_idx),
        resident(wna.shape),
        resident(wret.shape),
        resident(wout.shape),
    ]
    args = [o_na, u2, o_ret, u2, u2, u2, x2, gate, wna, wret, wout]
    stream = (row_tile(d, 0), jax.ShapeDtypeStruct((m, d), F32))
    nxt = (row_tile(d, 0), jax.ShapeDtypeStruct((m, d), BF16))
    if mode == "final":
        in_specs.append(pl.BlockSpec((1, d), lambda i: (0, 0)))
        args.append(extra)
        outs = [stream]
    else:
        g_next, scale_next, shift_next = extra
        in_specs += [pl.BlockSpec((1, d), lambda i: (0, 0)),
                     pl.BlockSpec((1, 1, d), mod_idx), pl.BlockSpec((1, 1, d), mod_idx)]
        args += [g_next, scale_next, shift_next]
        outs = [stream, nxt] if mode == "stream+next" else [nxt]
    return pl.pallas_call(
        functools.partial(_merge_kernel, mode=mode),
        grid=(m // tm,),
        in_specs=in_specs,
        out_specs=[o[0] for o in outs],
        out_shape=[o[1] for o in outs],
        scratch_shapes=[pltpu.VMEM((tm, d), F32)] if mode == "next" else [],
        compiler_params=pltpu.CompilerParams(
            dimension_semantics=("arbitrary",), vmem_limit_bytes=VMEM_LIMIT),
        name="merge",
    )(*args)


def _rope_tables(n_tokens):
    t = np.arange(n_tokens)
    row = (t // GRID_W).astype(np.float32)
    col = (t % GRID_W).astype(np.float32)
    n_freq = RET_KEY_DIM // 4
    inv_freq = (np.float32(ROPE_BASE) ** (-np.arange(n_freq, dtype=np.float32) / n_freq)).astype(np.float32)
    ang = np.concatenate([row[:, None] * inv_freq, col[:, None] * inv_freq], axis=-1)
    cos, sin = np.cos(ang), np.sin(ang)
    return (jnp.asarray(np.concatenate([cos, cos], axis=-1), F32),
            jnp.asarray(np.concatenate([-sin, sin], axis=-1), F32))


def kernel(x, c, ctx, c_ctx, ada_w, ada_b, norm_g, w_in, na_rpb, ret_decay_logit,
           w_proj_na, w_proj_ret, w_out, final_g):
    b, s, d = x.shape
    lc = ctx.shape[1]
    depth = ada_w.shape[0]

    c_all = jnp.concatenate([c, c_ctx[None], jnp.zeros((MOD_ROWS - b - 1, d), F32)], axis=0)
    mod = _modulation(c_all, ada_w, ada_b)
    rope = _rope_tables(s)
    cols = np.arange(IN_COLS)
    col_scale = jnp.asarray(
        np.where((cols >= OFF_NA_Q) & (cols < OFF_NA_K), NA_DIM ** -0.5 * LOG2E,
                 np.where((cols >= OFF_RET_K) & (cols < OFF_RET_V), RET_KEY_DIM ** -0.5, 1.0)
                 ).reshape(1, IN_COLS), F32)

    def modulation(l):
        shift = mod[l, :, 0:d].reshape(MOD_ROWS, 1, d)
        scale = mod[l, :, d:2 * d].reshape(MOD_ROWS, 1, d)
        gate = mod[l, :, 2 * d:3 * d].reshape(MOD_ROWS, 1, d)
        return norm_g[l].reshape(1, d), scale, shift, gate

    x_lat = x.reshape(b * s, d)
    x_ctx = ctx.reshape(b * lc, d)
    g, scale, shift, gate = modulation(0)
    h_lat = _prenorm(x_lat, shift, scale, g, rows_per_mod=s, mod_row=None, tm=PRENORM_TM)
    h_ctx = _prenorm(x_ctx, shift, scale, g, rows_per_mod=lc, mod_row=CTX_MOD_ROW, tm=PRENORM_TM)
    for l in range(depth):
        last = l == depth - 1
        u, wna, wret, wout = _inproj(h_lat, w_in, l, col_scale, rope, col_map=lambda j: j,
                                     n_out=IN_COLS, tm=INPROJ_TM, tn=INPROJ_TN,
                                     side_weights=(w_proj_na, w_proj_ret, w_out))
        if last:
            (uc,) = _inproj(h_ctx, w_in, l, col_scale, None,
                            col_map=lambda j: jnp.where(j < 2, j + 1, j + 3),
                            n_out=CTX_KV_COLS, tm=INPROJ_TM, tn=INPROJ_TN)
            offs = dict(na_k=COFF_NA_K, na_v=COFF_NA_V, ret_k=COFF_RET_K, ret_v=COFF_RET_V)
            na_q = ret_q = None
        else:
            (uc,) = _inproj(h_ctx, w_in, l, col_scale, None, col_map=lambda j: j, n_out=IN_COLS,
                            tm=INPROJ_TM, tn=INPROJ_TN)
            offs = dict(na_k=OFF_NA_K, na_v=OFF_NA_V, ret_k=OFF_RET_K, ret_v=OFF_RET_V)
            na_q, ret_q = OFF_NA_Q, OFF_RET_Q
        u3 = u.reshape(b, s, IN_COLS)
        uc3 = uc.reshape(b, lc, uc.shape[-1])

        o_na, o_na_ctx = _na_attention(_na_pair_bias(na_rpb[l]), u3, uc3,
                                       ctx_k_off=offs['na_k'], ctx_v_off=offs['na_v'], ctx_q_off=na_q)
        o_ret, o_ret_ctx = _retention(ret_decay_logit[l], u3, uc3,
                                      ctx_k_off=offs['ret_k'], ctx_v_off=offs['ret_v'], ctx_q_off=ret_q)

        lat_args = (o_na.reshape(b * s, -1), o_ret.reshape(b * s, -1), u, x_lat, gate, wna, wret, wout)
        if last:
            (x_lat,) = _merge(*lat_args, mode="final", extra=final_g.reshape(1, d),
                              rows_per_mod=s, mod_row=None, tm=MERGE_TM)
            break
        g, scale, shift, gate_next = modulation(l + 1)
        x_lat, h_lat = _merge(*lat_args, mode="stream+next", extra=(g, scale, shift),
                              rows_per_mod=s, mod_row=None, tm=MERGE_TM)
        ctx_args = (o_na_ctx.reshape(b * lc, -1), o_ret_ctx.reshape(b * lc, -1), uc, x_ctx,
                    gate, wna, wret, wout)
        if l + 1 == depth - 1:
            (h_ctx,) = _merge(*ctx_args, mode="next", extra=(g, scale, shift),
                              rows_per_mod=lc, mod_row=CTX_MOD_ROW, tm=MERGE_TM)
        else:
            x_ctx, h_ctx = _merge(*ctx_args, mode="stream+next", extra=(g, scale, shift),
                                  rows_per_mod=lc, mod_row=CTX_MOD_ROW, tm=MERGE_TM)
        gate = gate_next
    return x_lat.reshape(b, s, d)
```
